```python
import jax, jax.numpy as jnp
from jax import lax
import numpy as np

D_MODEL = 1024
BATCH = 16
SEQ = 2048
DEPTH = 1

CHUNK = 64
HEAD_DIM = 64
N_HEADS_GDN = 8
N_HEADS_RWKV = 8
D_GDN = N_HEADS_GDN * HEAD_DIM
D_RWKV = N_HEADS_RWKV * HEAD_DIM
CONV_WIDTH = 4
LORA_W = 64
LORA_A = 64
LORA_G = 128
D_FF = 2816
EPS = 1e-6
L2_EPS = 1e-6
GN_EPS = 64e-5
GDN_COLS = 4 * D_GDN + 2 * N_HEADS_GDN
RWKV_COLS = 3 * D_RWKV + LORA_W + LORA_A + LORA_G
D_IN = GDN_COLS + RWKV_COLS

kernel_name = "hymba_gdn_rwkv7_macaron_sandwich"


def _rmsnorm(x, g):
    x32 = x.astype(jnp.float32)
    y = x32 * lax.rsqrt(jnp.mean(x32 * x32, axis=-1, keepdims=True) + EPS)
    return (y * g.astype(jnp.float32)).astype(x.dtype)


def _l2norm(x):
    x32 = x.astype(jnp.float32)
    return x32 * lax.rsqrt(jnp.sum(x32 * x32, axis=-1, keepdims=True) + L2_EPS)


def _swiglu(x, w_gate, w_up, w_down):
    return (jax.nn.silu(x @ w_gate) * (x @ w_up)) @ w_down


def _token_shift(x):
    return jnp.pad(x, ((0, 0), (1, 0), (0, 0)))[:, :-1]


def _causal_depthwise_conv(x, w):
    k = w.shape[0]
    return lax.conv_general_dilated(
        x, w[:, None, :].astype(x.dtype), window_strides=(1,), padding=[(k - 1, 0)],
        dimension_numbers=('NWC', 'WIO', 'NWC'), feature_group_count=x.shape[-1])


def _gated_delta_rule_chunked(q, k, v, beta, g):
    B, T, H, D = q.shape
    N = T // CHUNK
    f32 = jnp.float32

    def to_chunks(t):
        t = t.astype(f32).reshape((B, N, CHUNK, H) + t.shape[3:])
        return jnp.moveaxis(t, 3, 1)

    q, k, v = to_chunks(q) * D ** -0.5, to_chunks(k), to_chunks(v)
    beta, g = to_chunks(beta), to_chunks(g)
    g = jnp.cumsum(g, axis=-1)
    idx = jnp.arange(CHUNK)
    causal = idx[:, None] >= idx[None, :]
    strict = idx[:, None] > idx[None, :]
    decay = jnp.exp(jnp.where(causal, g[..., :, None] - g[..., None, :], -jnp.inf))
    k_beta = k * beta[..., None]
    v_beta = v * beta[..., None]
    a_low = jnp.where(strict, jnp.einsum('bhncd,bhnsd->bhncs', k_beta, k) * decay, 0.0)
    eye = jnp.broadcast_to(jnp.eye(CHUNK, dtype=f32), a_low.shape)
    t_inv = lax.linalg.triangular_solve(a_low + eye, eye, left_side=True, lower=True,
                                        unit_diagonal=True)
    u = jnp.einsum('bhncs,bhnsd->bhncd', t_inv, v_beta)
    w = jnp.einsum('bhncs,bhnsd->bhncd', t_inv, k_beta * jnp.exp(g)[..., None])
    attn = jnp.where(causal, jnp.einsum('bhncd,bhnsd->bhncs', q, k) * decay, 0.0)

    def step(S, inp):
        q_c, k_c, u_c, w_c, g_c, attn_c = inp
        v_new = u_c - jnp.einsum('bhck,bhkv->bhcv', w_c, S)
        o = (jnp.einsum('bhck,bhkv->bhcv', q_c * jnp.exp(g_c)[..., None], S)
             + jnp.einsum('bhcs,bhsv->bhcv', attn_c, v_new))
        g_last = g_c[..., -1]
        S = (S * jnp.exp(g_last)[..., None, None]
             + jnp.einsum('bhck,bhcv->bhkv', k_c * jnp.exp(g_last[..., None] - g_c)[..., None], v_new))
        return S, o

    xs = tuple(jnp.moveaxis(t, 2, 0) for t in (q, k, u, w, g, attn))
    _, o = lax.scan(step, jnp.zeros((B, H, D, D), f32), xs)
    return jnp.transpose(o, (1, 0, 3, 2, 4)).reshape(B, T, H, D)


def _gdn_group(p, conv_w, a_log, dt_bias, norm_g):
    B, T, _ = p.shape
    f32 = jnp.float32
    qkv, z, b_logit, a_logit = jnp.split(p, [3 * D_GDN, 4 * D_GDN, 4 * D_GDN + N_HEADS_GDN], axis=-1)
    qkv = jax.nn.silu(_causal_depthwise_conv(qkv, conv_w))
    q, k, v = (t.reshape(B, T, N_HEADS_GDN, HEAD_DIM) for t in jnp.split(qkv, 3, axis=-1))
    q, k = _l2norm(q), _l2norm(k)
    beta = jax.nn.sigmoid(b_logit.astype(f32))
    g = -jnp.exp(a_log.astype(f32)) * jax.nn.softplus(a_logit.astype(f32) + dt_bias.astype(f32))
    o = _gated_delta_rule_chunked(q, k, v.astype(f32), beta, g)
    z = z.reshape(B, T, N_HEADS_GDN, HEAD_DIM).astype(f32)
    o = o * lax.rsqrt(jnp.mean(o * o, axis=-1, keepdims=True) + EPS) * norm_g.astype(f32) * jax.nn.silu(z)
    return o.reshape(B, T, D_GDN).astype(p.dtype)


def _rwkv7_scan(r, w, k, v, a, b):
    B, T, H, D = r.shape
    xs = tuple(jnp.moveaxis(t, 1, 0) for t in (r, w, k, v, a, b))

    def step(S, inp):
        r_t, w_t, k_t, v_t, a_t, b_t = inp
        sa = jnp.einsum('bhvk,bhk->bhv', S, a_t)
        S = S * w_t[:, :, None, :] + sa[..., None] * b_t[:, :, None, :] + v_t[..., None] * k_t[:, :, None, :]
        return S, jnp.einsum('bhvk,bhk->bhv', S, r_t)

    _, y = lax.scan(step, jnp.zeros((B, H, D, D), jnp.float32), xs)
    return jnp.moveaxis(y, 0, 1)


def _rwkv7_group(p, mu, w0, w2, a0, a2, g2, k_k, k_a, r_k, ln_g, ln_b):
    B, T, _ = p.shape
    f32 = jnp.float32
    H, D = N_HEADS_RWKV, HEAD_DIM
    p = p + (_token_shift(p) - p) * mu
    r, k, v, w_lo, a_lo, g_lo = jnp.split(
        p, [D_RWKV, 2 * D_RWKV, 3 * D_RWKV, 3 * D_RWKV + LORA_W, 3 * D_RWKV + LORA_W + LORA_A], axis=-1)
    w = -jax.nn.softplus(-(w0 + jnp.tanh(w_lo) @ w2)) - 0.5
    decay = jnp.exp(-jnp.exp(w.astype(f32)))
    a = jax.nn.sigmoid(a0 + a_lo @ a2)
    gate = jax.nn.sigmoid(g_lo) @ g2
    kk = _l2norm((k * k_k).reshape(B, T, H, D))
    k = k * (1 + (a - 1) * k_a)
    heads = lambda t: t.astype(f32).reshape(B, T, H, D)
    r_h, k_h, v_h, a_h, w_h = heads(r), heads(k), heads(v), heads(a), heads(decay)
    y = _rwkv7_scan(r_h, w_h, k_h, v_h, -kk, kk * a_h)
    mean = jnp.mean(y, axis=-1, keepdims=True)
    yc = y - mean
    y = yc * lax.rsqrt(jnp.mean(yc * yc, axis=-1, keepdims=True) + GN_EPS)
    y = y.reshape(B, T, D_RWKV) * ln_g.astype(f32) + ln_b.astype(f32)
    bonus = jnp.sum(r_h * k_h * r_k.astype(f32), axis=-1, keepdims=True) * v_h
    y = (y + bonus.reshape(B, T, D_RWKV)) * gate.astype(f32)
    return y.astype(p.dtype)


def setup_inputs(seed: int = 0) -> dict:
    key = jax.random.key(seed)
    ks = iter(jax.random.split(key, 40))
    L = DEPTH
    nrm = lambda shape, scale: jax.random.normal(next(ks), shape, jnp.float32) * scale
    gain = lambda shape: 1.0 + nrm(shape, 0.02)
    unif = lambda shape, lo, hi: jax.random.uniform(next(ks), shape, jnp.float32, lo, hi)
    x = nrm((BATCH, SEQ, D_MODEL), 1.0)
    dt = jnp.exp(unif((L, N_HEADS_GDN), float(np.log(1e-3)), float(np.log(1e-1))))
    return {
        "x": x,
        "ffn1_pre_g": gain((L, D_MODEL)),
        "ffn1_w_gate": nrm((L, D_MODEL, D_FF), D_MODEL ** -0.5),
        "ffn1_w_up": nrm((L, D_MODEL, D_FF), D_MODEL ** -0.5),
        "ffn1_w_down": nrm((L, D_FF, D_MODEL), D_FF ** -0.5),
        "ffn1_post_g": gain((L, D_MODEL)),
        "mix_pre_g": gain((L, D_MODEL)),
        "w_in": nrm((L, D_MODEL, D_IN), D_MODEL ** -0.5),
        "gdn_conv_w": nrm((L, CONV_WIDTH, 3 * D_GDN), CONV_WIDTH ** -0.5),
        "gdn_a_log": jnp.log(unif((L, N_HEADS_GDN), 1.0, 16.0)),
        "gdn_dt_bias": dt + jnp.log(-jnp.expm1(-dt)),
        "gdn_norm_g": gain((L, HEAD_DIM)),
        "rwkv_mu": unif((L, RWKV_COLS), 0.0, 1.0),
        "rwkv_w0": unif((L, D_RWKV), -5.0, 0.0),
        "rwkv_w2": nrm((L, LORA_W, D_RWKV), 0.1 * LORA_W ** -0.5),
        "rwkv_a0": nrm((L, D_RWKV), 0.1),
        "rwkv_a2": nrm((L, LORA_A, D_RWKV), 0.1 * LORA_A ** -0.5),
        "rwkv_g2": nrm((L, LORA_G, D_RWKV), LORA_G ** -0.5),
        "rwkv_k_k": 0.85 + nrm((L, D_RWKV), 0.02),
        "rwkv_k_a": gain((L, D_RWKV)),
        "rwkv_r_k": nrm((L, N_HEADS_RWKV, HEAD_DIM), 0.1),
        "rwkv_ln_g": gain((L, D_RWKV)),
        "rwkv_ln_b": nrm((L, D_RWKV), 0.02),
        "w_out": nrm((L, D_GDN + D_RWKV, D_MODEL), (D_GDN + D_RWKV) ** -0.5),
        "mix_post_g": gain((L, D_MODEL)),
        "ffn2_pre_g": gain((L, D_MODEL)),
        "ffn2_w_gate": nrm((L, D_MODEL, D_FF), D_MODEL ** -0.5),
        "ffn2_w_up": nrm((L, D_MODEL, D_FF), D_MODEL ** -0.5),
        "ffn2_w_down": nrm((L, D_FF, D_MODEL), D_FF ** -0.5),
        "ffn2_post_g": gain((L, D_MODEL)),
    }


def reference(x, ffn1_pre_g, ffn1_w_gate, ffn1_w_up, ffn1_w_down, ffn1_post_g,
              mix_pre_g, w_in, gdn_conv_w, gdn_a_log, gdn_dt_bias, gdn_norm_g,
              rwkv_mu, rwkv_w0, rwkv_w2, rwkv_a0, rwkv_a2, rwkv_g2, rwkv_k_k, rwkv_k_a,
              rwkv_r_k, rwkv_ln_g, rwkv_ln_b, w_out, mix_post_g,
              ffn2_pre_g, ffn2_w_gate, ffn2_w_up, ffn2_w_down, ffn2_post_g):
    h = x
    for l in range(DEPTH):
        f = _swiglu(_rmsnorm(h, ffn1_pre_g[l]), ffn1_w_gate[l], ffn1_w_up[l], ffn1_w_down[l])
        h = h + 0.5 * _rmsnorm(f, ffn1_post_g[l])
        proj = _rmsnorm(h, mix_pre_g[l]) @ w_in[l]
        y_gdn = _gdn_group(proj[..., :GDN_COLS], gdn_conv_w[l], gdn_a_log[l], gdn_dt_bias[l], gdn_norm_g[l])
        y_rwkv = _rwkv7_group(proj[..., GDN_COLS:], rwkv_mu[l], rwkv_w0[l], rwkv_w2[l], rwkv_a0[l],
                              rwkv_a2[l], rwkv_g2[l], rwkv_k_k[l], rwkv_k_a[l], rwkv_r_k[l],
                              rwkv_ln_g[l], rwkv_ln_b[l])
        mix = jnp.concatenate([y_gdn, y_rwkv], axis=-1) @ w_out[l]
        h = h + _rmsnorm(mix, mix_post_g[l])
        f = _swiglu(_rmsnorm(h, ffn2_pre_g[l]), ffn2_w_gate[l], ffn2_w_up[l], ffn2_w_down[l])
        h = h + 0.5 * _rmsnorm(f, ffn2_post_g[l])
    return h
```

```python
import functools

import jax
import jax.numpy as jnp
from jax import lax
from jax.experimental import pallas as pl
from jax.experimental.pallas import tpu as pltpu

F32 = jnp.float32
BF16 = jnp.bfloat16

D_MODEL = 1024
D_FF = 2816
HEAD_DIM = 64
N_HEADS = 8
D_MIX = N_HEADS * HEAD_DIM
N_PAIRS = N_HEADS // 2
PAIR = 2 * HEAD_DIM
CHUNK = 64
CONV_WIDTH = 4
LORA_W = 64
LORA_A = 64
LORA_G = 128
EPS = 1e-6
L2_EPS = 1e-6
GN_EPS = 64e-5

LANES = 128
SUBLANES = 8
GDN_MAIN = 4 * D_MIX
GDN_SMALL = LANES
RWKV_COLS = 3 * D_MIX + LORA_W + LORA_A + LORA_G

FFN_TM = 512
FFN_TF = 1408
PROJ_TM = 512
MIX_TB = 256
VMEM_LIMIT = 56 * 1024 * 1024


def _dot(a, b):
    return jnp.dot(a.astype(BF16), b.astype(BF16), preferred_element_type=F32)


def _dot_nt(a, b):
    return lax.dot_general(a.astype(BF16), b.astype(BF16), (((1,), (1,)), ((), ())),
                           preferred_element_type=F32)


def _dot_tn(a, b):
    return lax.dot_general(a.astype(BF16), b.astype(BF16), (((0,), (0,)), ((), ())),
                           preferred_element_type=F32)


def _split(x, passes):
    pieces = []
    rem = x
    for i in range(passes):
        p = rem.astype(BF16)
        pieces.append(p)
        if i + 1 < passes:
            rem = rem - p.astype(F32)
    return pieces


def _dot_sel(x, sel, passes=3):
    out = None
    for p in _split(x, passes):
        t = jnp.dot(p, sel, preferred_element_type=F32)
        out = t if out is None else out + t
    return out


def _sel_dot(sel, x, passes=3):
    out = None
    for p in _split(x, passes):
        t = jnp.dot(sel, p, preferred_element_type=F32)
        out = t if out is None else out + t
    return out


def _rmsnorm(x, g):
    return x * lax.rsqrt(jnp.mean(x * x, axis=-1, keepdims=True) + EPS) * g


def _sigmoid(x):
    return jax.nn.sigmoid(x)


def _silu(x):
    return x * jax.nn.sigmoid(x)


def _softplus(x):
    return jnp.maximum(x, 0.0) + jnp.log1p(jnp.exp(-jnp.abs(x)))


def _bd(x):
    x = x.astype(BF16)
    lane = lax.broadcasted_iota(jnp.int32, x.shape, 1)
    zero = jnp.zeros_like(x)
    return jnp.concatenate([jnp.where(lane < HEAD_DIM, x, zero),
                            jnp.where(lane >= HEAD_DIM, x, zero)], axis=0)


def _unbd(x):
    return x[:CHUNK] + x[CHUNK:]


def _tri_inverse(x, ii, jj):
    def sub_blocks(b):
        return ((ii ^ jj) < 2 * b) & ((ii & b) != 0) & ((jj & b) == 0)

    t = jnp.where(ii == jj, 1.0, 0.0) - jnp.where(sub_blocks(1), x, 0.0)
    b = 2
    while b < CHUNK:
        lb = jnp.where(sub_blocks(b), x, 0.0)
        t = t - _dot(_dot(t, lb), t)
        b *= 2
    return t


def _pair_masks():
    ii = lax.broadcasted_iota(jnp.int32, (PAIR, PAIR), 0)
    jj = lax.broadcasted_iota(jnp.int32, (PAIR, PAIR), 1)
    same_head = (ii ^ jj) < HEAD_DIM
    return ii, jj, same_head & (ii >= jj), same_head & (ii > jj)


def _ffn_body(with_mix, *refs):
    if with_mix:
        (yg_ref, yr_ref, h_ref, wout_ref, mixg_ref, preg_ref, wg_ref, wu_ref, wd_ref, postg_ref,
         o_ref, xn_scr, h_scr, acc_scr) = refs
    else:
        (h_ref, preg_ref, wg_ref, wu_ref, wd_ref, postg_ref, o_ref, xn_scr, h_scr, acc_scr) = refs
    j = pl.program_id(1)

    @pl.when(j == 0)
    def _():
        h = h_ref[...]
        if with_mix:
            y = jnp.concatenate([yg_ref[...], yr_ref[...]], axis=-1)
            mix = _dot(y, wout_ref[...])
            h = h + _rmsnorm(mix, mixg_ref[...])
        h_scr[...] = h
        xn_scr[...] = _rmsnorm(h, preg_ref[...]).astype(BF16)
        acc_scr[...] = jnp.zeros_like(acc_scr)

    xn = xn_scr[...]
    gate = jnp.dot(xn, wg_ref[...], preferred_element_type=F32)
    up = jnp.dot(xn, wu_ref[...], preferred_element_type=F32)
    hid = (_silu(gate) * up).astype(BF16)
    acc_scr[...] += jnp.dot(hid, wd_ref[...], preferred_element_type=F32)

    @pl.when(j == pl.num_programs(1) - 1)
    def _():
        o_ref[...] = h_scr[...] + 0.5 * _rmsnorm(acc_scr[...], postg_ref[...])


def _ffn(h, pre_g, w_gate, w_up, w_down, post_g, mix=None):
    m = h.shape[0]
    grid = (m // FFN_TM, D_FF // FFN_TF)
    row = lambda width: pl.BlockSpec((FFN_TM, width), lambda i, j: (i, 0))
    const = lambda shape: pl.BlockSpec(shape, lambda i, j: (0, 0))
    in_specs, args = [], []
    if mix is not None:
        y_gdn, y_rwkv, w_out, mix_g = mix
        in_specs += [row(D_MIX), row(D_MIX)]
        args += [y_gdn, y_rwkv]
    in_specs.append(row(D_MODEL))
    args.append(h)
    if mix is not None:
        in_specs += [const((D_MODEL, D_MODEL)), const((1, D_MODEL))]
        args += [w_out, mix_g]
    in_specs += [const((1, D_MODEL)),
                 pl.BlockSpec((D_MODEL, FFN_TF), lambda i, j: (0, j)),
                 pl.BlockSpec((D_MODEL, FFN_TF), lambda i, j: (0, j)),
                 pl.BlockSpec((FFN_TF, D_MODEL), lambda i, j: (j, 0)),
                 const((1, D_MODEL))]
    args += [pre_g, w_gate, w_up, w_down, post_g]
    return pl.pallas_call(
        functools.partial(_ffn_body, mix is not None),
        grid=grid,
        in_specs=in_specs,
        out_specs=row(D_MODEL),
        out_shape=jax.ShapeDtypeStruct((m, D_MODEL), F32),
        scratch_shapes=[pltpu.VMEM((FFN_TM, D_MODEL), BF16),
                        pltpu.VMEM((FFN_TM, D_MODEL), F32),
                        pltpu.VMEM((FFN_TM, D_MODEL), F32)],
        compiler_params=pltpu.CompilerParams(
            dimension_semantics=("parallel", "arbitrary"), vmem_limit_bytes=VMEM_LIMIT),
        name="ffn_mix" if mix is not None else "ffn",
    )(*args)


def _proj_body(h_ref, g_ref, w_ref, main_ref, small_ref, rwkv_ref):
    xn = _rmsnorm(h_ref[...], g_ref[...]).astype(BF16)
    p = jnp.dot(xn, w_ref[...], preferred_element_type=F32)
    main_ref[...] = p[:, :GDN_MAIN]
    small_ref[...] = p[:, GDN_MAIN:GDN_MAIN + GDN_SMALL]
    rwkv_ref[...] = p[:, GDN_MAIN + GDN_SMALL:]


def _in_proj(h, g, w):
    m = h.shape[0]
    n = w.shape[1]
    row = lambda width: pl.BlockSpec((PROJ_TM, width), lambda i: (i, 0))
    return pl.pallas_call(
        _proj_body,
        grid=(m // PROJ_TM,),
        in_specs=[row(D_MODEL), pl.BlockSpec((1, D_MODEL), lambda i: (0, 0)),
                  pl.BlockSpec((D_MODEL, n), lambda i: (0, 0))],
        out_specs=[row(GDN_MAIN), row(GDN_SMALL), row(RWKV_COLS)],
        out_shape=[jax.ShapeDtypeStruct((m, GDN_MAIN), F32),
                   jax.ShapeDtypeStruct((m, GDN_SMALL), F32),
                   jax.ShapeDtypeStruct((m, RWKV_COLS), F32)],
        compiler_params=pltpu.CompilerParams(
            dimension_semantics=("parallel",), vmem_limit_bytes=VMEM_LIMIT),
        name="in_proj",
    )(h, g, w)


def _gdn_chunk_pair(q, k, v, beta, g_nat, g_col, s, masks):
    ii, jj, causal, strict = masks
    g_last = g_nat[CHUNK - 1:CHUNK, :]
    e_g = jnp.exp(g_nat)
    kb = k * beta
    decay = jnp.exp(jnp.where(causal, g_col - g_col.T, -jnp.inf))
    k_bd = _bd(k)
    gram = _dot_nt(jnp.concatenate([_bd(kb), _bd(q)], axis=0), k_bd)
    a = jnp.where(strict, gram[:PAIR] * decay, 0.0)
    attn = gram[PAIR:] * decay
    t = _tri_inverse(a, ii, jj)
    uw = _dot(t, jnp.concatenate([_bd(v * beta), _bd(kb * e_g)], axis=1))
    ws_qs = _dot(jnp.concatenate([uw[:, PAIR:].astype(BF16), _bd(q * e_g)], axis=0), s)
    v_new = uw[:, :PAIR] - ws_qs[:PAIR]
    o = ws_qs[PAIR:] + _dot(attn, v_new)
    g_last_col = jnp.concatenate(
        [jnp.broadcast_to(g_col[CHUNK - 1:CHUNK, :], (CHUNK, PAIR)),
         jnp.broadcast_to(g_col[PAIR - 1:PAIR, :], (CHUNK, PAIR))], axis=0)
    s_new = s * jnp.exp(g_last_col) + _dot_tn(_bd(k * jnp.exp(g_last - g_nat)), v_new)
    return _unbd(o), s_new


def _gdn_body(xm_ref, xs_ref, convw_ref, alog_ref, dtb_ref, ng_ref, lblk_ref, e512_ref, eb_ref,
              eg_ref, egall_ref, o_ref, xbuf, q_s, k_s, v_s, beta_s, g_s, gall_s, o_s, state):
    tb = MIX_TB

    @pl.when(pl.program_id(1) == 0)
    def _():
        xbuf[0:SUBLANES, :] = jnp.zeros((SUBLANES, 3 * D_MIX), F32)
        state[...] = jnp.zeros_like(state)

    xbuf[SUBLANES:SUBLANES + tb, :] = xm_ref[0, :, 0:3 * D_MIX]
    conv = None
    for j in range(CONV_WIDTH):
        term = xbuf[pl.ds(SUBLANES - (CONV_WIDTH - 1) + j, tb), :] * convw_ref[j:j + 1, :]
        conv = term if conv is None else conv + term
    act = _silu(conv)
    e512 = e512_ref[...]
    q = act[:, 0:D_MIX]
    k = act[:, D_MIX:2 * D_MIX]
    q_s[...] = q * lax.rsqrt(_dot_sel(q * q, e512, 2) + L2_EPS) * (HEAD_DIM ** -0.5)
    k_s[...] = k * lax.rsqrt(_dot_sel(k * k, e512, 2) + L2_EPS)
    v_s[...] = act[:, 2 * D_MIX:3 * D_MIX]

    xs = xs_ref[0]
    beta_s[...] = _dot_sel(_sigmoid(xs), eb_ref[...])
    g_small = -jnp.exp(alog_ref[...]) * _softplus(xs + dtb_ref[...])
    g_cum = _sel_dot(lblk_ref[...], g_small)
    g_s[...] = _dot_sel(g_cum, eg_ref[...])
    gall_s[...] = _dot_sel(g_cum, egall_ref[...])

    masks = _pair_masks()

    def chunk(c, carry):
        r0 = pl.multiple_of(c * CHUNK, CHUNK)
        rows = pl.ds(r0, CHUNK)
        for p in range(N_PAIRS):
            cols = pl.ds(p * PAIR, PAIR)
            g_col = jnp.concatenate([gall_s[rows, pl.ds(2 * p * LANES, LANES)],
                                     gall_s[rows, pl.ds((2 * p + 1) * LANES, LANES)]], axis=0)
            o, s_new = _gdn_chunk_pair(q_s[rows, cols], k_s[rows, cols], v_s[rows, cols],
                                       beta_s[rows, cols], g_s[rows, cols], g_col, state[p], masks)
            state[p] = s_new
            o_s[rows, cols] = o
        return carry

    lax.fori_loop(0, tb // CHUNK, chunk, 0)

    o = o_s[...]
    z = xm_ref[0, :, 3 * D_MIX:4 * D_MIX]
    ms = _dot_sel(o * o, e512, 2) * (1.0 / HEAD_DIM)
    o_ref[0] = o * lax.rsqrt(ms + EPS) * ng_ref[...] * _silu(z)
    xbuf[0:SUBLANES, :] = xbuf[tb:tb + SUBLANES, :]


def _gdn(x_main, x_small, conv_w, a_log_pad, dt_bias_pad, norm_g_full, consts):
    b, t, _ = x_main.shape
    tb = MIX_TB
    lblk, e512, eb, eg, egall = consts
    const = lambda arr: pl.BlockSpec(arr.shape, lambda i, j: (0,) * arr.ndim)
    smalls = [conv_w, a_log_pad, dt_bias_pad, norm_g_full, lblk, e512, eb, eg, egall]
    return pl.pallas_call(
        _gdn_body,
        grid=(b, t // tb),
        in_specs=[pl.BlockSpec((1, tb, GDN_MAIN), lambda i, j: (i, j, 0)),
                  pl.BlockSpec((1, tb, GDN_SMALL), lambda i, j: (i, j, 0))] + [const(a) for a in smalls],
        out_specs=pl.BlockSpec((1, tb, D_MIX), lambda i, j: (i, j, 0)),
        out_shape=jax.ShapeDtypeStruct((b, t, D_MIX), F32),
        scratch_shapes=[pltpu.VMEM((tb + SUBLANES, 3 * D_MIX), F32)]
        + [pltpu.VMEM((tb, D_MIX), F32)] * 5
        + [pltpu.VMEM((tb, N_HEADS * LANES), F32),
           pltpu.VMEM((tb, D_MIX), F32),
           pltpu.VMEM((N_PAIRS, PAIR, PAIR), F32)],
        compiler_params=pltpu.CompilerParams(
            dimension_semantics=("parallel", "arbitrary"), vmem_limit_bytes=VMEM_LIMIT),
        name="gdn",
    )(x_main, x_small, *smalls)


def _rwkv_chunk_pair(r, k, v, aa, bb, lw, lp, s, masks):
    ii, jj, causal, strict = masks
    p_fwd = jnp.exp(lp)
    p_inv = jnp.exp(-lp)
    lp_last = lp[CHUNK - 1:CHUNK, :]
    p_rest = jnp.exp(lp_last - lp)
    a_h = _bd(aa * jnp.exp(lp - lw))
    r_h = _bd(r * p_fwd)
    v_bd = _bd(v)
    gram = _dot_nt(jnp.concatenate([a_h, r_h], axis=0),
                   jnp.concatenate([_bd(bb * p_inv), _bd(k * p_inv)], axis=0))
    a_ab = jnp.where(strict, gram[:PAIR, :PAIR], 0.0)
    a_ak = jnp.where(strict, gram[:PAIR, PAIR:], 0.0)
    a_rb = jnp.where(causal, gram[PAIR:, :PAIR], 0.0)
    a_rk = jnp.where(causal, gram[PAIR:, PAIR:], 0.0)
    t = _tri_inverse(-a_ab, ii, jj)
    av = _dot(jnp.concatenate([a_ak, a_rk], axis=0), v_bd)
    wu = _dot(t, jnp.concatenate([a_h, av[:PAIR].astype(BF16)], axis=1))
    ws_rs = _dot_nt(jnp.concatenate([wu[:, :PAIR].astype(BF16), r_h], axis=0), s)
    u = ws_rs[:PAIR] + wu[:, PAIR:]
    y = ws_rs[PAIR:] + _dot(a_rb, u) + av[PAIR:]
    s_new = s * jnp.exp(lp_last) + _dot_tn(
        jnp.concatenate([u.astype(BF16), v_bd], axis=0),
        jnp.concatenate([_bd(bb * p_rest), _bd(k * p_rest)], axis=0))
    return _unbd(y), s_new


def _rwkv_body(p_ref, mu_ref, w0_ref, a0_ref, w2_ref, a2_ref, g2_ref, kk_ref, ka_ref, rk_ref,
               lng_ref, lnb_ref, lblk_ref, e512_ref, o_ref,
               pbuf, r_s, k_s, v_s, aa_s, bb_s, lw_s, lp_s, gate_s, y_s, state):
    tb = MIX_TB

    @pl.when(pl.program_id(1) == 0)
    def _():
        pbuf[0:SUBLANES, :] = jnp.zeros((SUBLANES, RWKV_COLS), F32)
        state[...] = jnp.zeros_like(state)

    pbuf[SUBLANES:SUBLANES + tb, :] = p_ref[0]
    cur = pbuf[SUBLANES:SUBLANES + tb, :]
    prev = pbuf[pl.ds(SUBLANES - 1, tb), :]
    p = cur + (prev - cur) * mu_ref[...]
    r = p[:, 0:D_MIX]
    k = p[:, D_MIX:2 * D_MIX]
    v = p[:, 2 * D_MIX:3 * D_MIX]
    lora_wa = p[:, 3 * D_MIX:3 * D_MIX + LORA_W + LORA_A]
    g_lo = p[:, 3 * D_MIX + LORA_W + LORA_A:]
    e512 = e512_ref[...]

    w = -_softplus(-(w0_ref[...] + _dot(jnp.tanh(lora_wa), w2_ref[...]))) - 0.5
    lw = -jnp.exp(w)
    a = _sigmoid(a0_ref[...] + _dot(lora_wa, a2_ref[...]))
    gate_s[...] = _dot(_sigmoid(g_lo), g2_ref[...])
    kk = k * kk_ref[...]
    kk = kk * lax.rsqrt(_dot_sel(kk * kk, e512, 2) + L2_EPS)
    k = k * (1.0 + (a - 1.0) * ka_ref[...])
    r_s[...] = r
    k_s[...] = k
    v_s[...] = v
    aa_s[...] = -kk
    bb_s[...] = kk * a
    lw_s[...] = lw
    lp_s[...] = _sel_dot(lblk_ref[...], lw)

    masks = _pair_masks()

    def chunk(c, carry):
        r0 = pl.multiple_of(c * CHUNK, CHUNK)
        rows = pl.ds(r0, CHUNK)
        for pr in range(N_PAIRS):
            cols = pl.ds(pr * PAIR, PAIR)
            y, s_new = _rwkv_chunk_pair(r_s[rows, cols], k_s[rows, cols], v_s[rows, cols],
                                        aa_s[rows, cols], bb_s[rows, cols], lw_s[rows, cols],
                                        lp_s[rows, cols], state[pr], masks)
            state[pr] = s_new
            y_s[rows, cols] = y
        return carry

    lax.fori_loop(0, tb // CHUNK, chunk, 0)

    y = y_s[...]
    inv_d = 1.0 / HEAD_DIM
    yc = y - _dot_sel(y, e512, 2) * inv_d
    y = yc * lax.rsqrt(_dot_sel(yc * yc, e512, 2) * inv_d + GN_EPS) * lng_ref[...] + lnb_ref[...]
    bonus = _dot_sel(r_s[...] * k_s[...] * rk_ref[...], e512, 2) * v_s[...]
    o_ref[0] = (y + bonus) * gate_s[...]
    pbuf[0:SUBLANES, :] = pbuf[tb:tb + SUBLANES, :]


def _rwkv(p, params, consts):
    b, t, _ = p.shape
    tb = MIX_TB
    lblk, e512 = consts
    const = lambda arr: pl.BlockSpec(arr.shape, lambda i, j: (0,) * arr.ndim)
    smalls = list(params) + [lblk, e512]
    return pl.pallas_call(
        _rwkv_body,
        grid=(b, t // tb),
        in_specs=[pl.BlockSpec((1, tb, RWKV_COLS), lambda i, j: (i, j, 0))] + [const(a) for a in smalls],
        out_specs=pl.BlockSpec((1, tb, D_MIX), lambda i, j: (i, j, 0)),
        out_shape=jax.ShapeDtypeStruct((b, t, D_MIX), F32),
        scratch_shapes=[pltpu.VMEM((tb + SUBLANES, RWKV_COLS), F32)]
        + [pltpu.VMEM((tb, D_MIX), F32)] * 9
        + [pltpu.VMEM((N_PAIRS, PAIR, PAIR), F32)],
        compiler_params=pltpu.CompilerParams(
            dimension_semantics=("parallel", "arbitrary"), vmem_limit_bytes=VMEM_LIMIT),
        name="rwkv",
    )(p, *smalls)


def _selection_constants(tb):
    i = jnp.arange(tb)
    lblk = ((i[:, None] // CHUNK == i[None, :] // CHUNK) & (i[:, None] >= i[None, :])).astype(BF16)
    c = jnp.arange(D_MIX)
    e512 = (c[:, None] // HEAD_DIM == c[None, :] // HEAD_DIM).astype(BF16)
    row = jnp.arange(GDN_SMALL)
    eb = (row[:, None] == c[None, :] // HEAD_DIM).astype(BF16)
    eg = (row[:, None] == N_HEADS + c[None, :] // HEAD_DIM).astype(BF16)
    ca = jnp.arange(N_HEADS * LANES)
    egall = (row[:, None] == N_HEADS + ca[None, :] // LANES).astype(BF16)
    return lblk, e512, eb, eg, egall


def kernel(x, ffn1_pre_g, ffn1_w_gate, ffn1_w_up, ffn1_w_down, ffn1_post_g, mix_pre_g, w_in, gdn_conv_w, gdn_a_log, gdn_dt_bias, gdn_norm_g, rwkv_mu, rwkv_w0, rwkv_w2, rwkv_a0, rwkv_a2, rwkv_g2, rwkv_k_k, rwkv_k_a, rwkv_r_k, rwkv_ln_g, rwkv_ln_b, w_out, mix_post_g, ffn2_pre_g, ffn2_w_gate, ffn2_w_up, ffn2_w_down, ffn2_post_g):
    b, t, d = x.shape
    depth = ffn1_pre_g.shape[0]
    lblk, e512, eb, eg, egall = _selection_constants(MIX_TB)
    row = lambda v: v.reshape(1, -1).astype(F32)
    h = x.reshape(b * t, d)
    for l in range(depth):
        h = _ffn(h, row(ffn1_pre_g[l]), ffn1_w_gate[l].astype(BF16), ffn1_w_up[l].astype(BF16),
                 ffn1_w_down[l].astype(BF16), row(ffn1_post_g[l]))

        wl = w_in[l]
        n_qkvz = 4 * D_MIX
        w_cat = jnp.concatenate(
            [wl[:, :n_qkvz], wl[:, n_qkvz:n_qkvz + 2 * N_HEADS],
             jnp.zeros((d, GDN_SMALL - 2 * N_HEADS), wl.dtype), wl[:, n_qkvz + 2 * N_HEADS:]], axis=1)
        x_main, x_small, x_rwkv = _in_proj(h, row(mix_pre_g[l]), w_cat.astype(BF16))

        pad_small = lambda v: jnp.zeros((1, GDN_SMALL), F32).at[0, N_HEADS:2 * N_HEADS].set(v)
        y_gdn = _gdn(x_main.reshape(b, t, GDN_MAIN), x_small.reshape(b, t, GDN_SMALL),
                     gdn_conv_w[l].astype(F32), pad_small(gdn_a_log[l]), pad_small(gdn_dt_bias[l]),
                     row(jnp.tile(gdn_norm_g[l], N_HEADS)), (lblk, e512, eb, eg, egall))

        zeros_lora = jnp.zeros((LORA_W, D_MIX), F32)
        w2_pad = jnp.concatenate([rwkv_w2[l], zeros_lora], axis=0).astype(BF16)
        a2_pad = jnp.concatenate([zeros_lora, rwkv_a2[l]], axis=0).astype(BF16)
        rwkv_params = (row(rwkv_mu[l]), row(rwkv_w0[l]), row(rwkv_a0[l]), w2_pad, a2_pad,
                       rwkv_g2[l].astype(BF16), row(rwkv_k_k[l]), row(rwkv_k_a[l]), row(rwkv_r_k[l]),
                       row(rwkv_ln_g[l]), row(rwkv_ln_b[l]))
        y_rwkv = _rwkv(x_rwkv.reshape(b, t, RWKV_COLS), rwkv_params, (lblk, e512))

        h = _ffn(h, row(ffn2_pre_g[l]), ffn2_w_gate[l].astype(BF16), ffn2_w_up[l].astype(BF16),
                 ffn2_w_down[l].astype(BF16), row(ffn2_post_g[l]),
                 mix=(y_gdn.reshape(b * t, D_MIX), y_rwkv.reshape(b * t, D_MIX),
                      w_out[l].astype(BF16), row(mix_post_g[l])))
    return h.reshape(b, t, d)
```

```python
import functools

import jax
import jax.numpy as jnp
from jax import lax
from jax.experimental import pallas as pl
from jax.experimental.pallas import tpu as pltpu

F32 = jnp.float32
BF16 = jnp.bfloat16

D_MODEL = 1024
D_FF = 2816
HEAD_DIM = 64
N_HEADS = 8
D_MIX = N_HEADS * HEAD_DIM
N_PAIRS = N_HEADS // 2
PAIR = 2 * HEAD_DIM
CHUNK = 64
CONV_WIDTH = 4
LORA_W = 64
LORA_A = 64
LORA_G = 128
EPS = 1e-6
L2_EPS = 1e-6
GN_EPS = 64e-5

LANES = 128
SUBLANES = 8
GDN_MAIN = 4 * D_MIX
GDN_SMALL = LANES
RWKV_COLS = 3 * D_MIX + LORA_W + LORA_A + LORA_G

FFN_TM = 512
FFN_TF = 1408
PROJ_TM = 512
MIX_TB = 256
VMEM_LIMIT = 56 * 1024 * 1024


def _dot(a, b):
    return jnp.dot(a.astype(BF16), b.astype(BF16), preferred_element_type=F32)


def _dot_nt(a, b):
    return lax.dot_general(a.astype(BF16), b.astype(BF16), (((1,), (1,)), ((), ())),
                           preferred_element_type=F32)


def _dot_tn(a, b):
    return lax.dot_general(a.astype(BF16), b.astype(BF16), (((0,), (0,)), ((), ())),
                           preferred_element_type=F32)


def _split(x, passes):
    pieces = []
    rem = x
    for i in range(passes):
        p = rem.astype(BF16)
        pieces.append(p)
        if i + 1 < passes:
            rem = rem - p.astype(F32)
    return pieces


def _dot_sel(x, sel, passes=3):
    out = None
    for p in _split(x, passes):
        t = jnp.dot(p, sel, preferred_element_type=F32)
        out = t if out is None else out + t
    return out


def _sel_dot(sel, x, passes=3):
    out = None
    for p in _split(x, passes):
        t = jnp.dot(sel, p, preferred_element_type=F32)
        out = t if out is None else out + t
    return out


def _rmsnorm(x, g):
    return x * lax.rsqrt(jnp.mean(x * x, axis=-1, keepdims=True) + EPS) * g


def _sigmoid(x):
    return jax.nn.sigmoid(x)


def _silu(x):
    return x * jax.nn.sigmoid(x)


def _softplus(x):
    return jnp.maximum(x, 0.0) + jnp.log1p(jnp.exp(-jnp.abs(x)))


def _bd(x, dtype=BF16):
    x = x.astype(dtype)
    lane = lax.broadcasted_iota(jnp.int32, x.shape, 1)
    zero = jnp.zeros_like(x)
    return jnp.concatenate([jnp.where(lane < HEAD_DIM, x, zero),
                            jnp.where(lane >= HEAD_DIM, x, zero)], axis=0)


def _unbd(x):
    return x[:CHUNK] + x[CHUNK:]


def _tri_inverse_many(xs, ii, jj):
    def sub_blocks(b):
        return ((ii ^ jj) < 2 * b) & ((ii & b) != 0) & ((jj & b) == 0)

    eye = jnp.where(ii == jj, 1.0, 0.0)
    first = sub_blocks(1)
    ts = [eye - jnp.where(first, x, 0.0) for x in xs]
    b = 2
    while b < CHUNK:
        m = sub_blocks(b)
        tbs = [t.astype(BF16) for t in ts]
        tls = [jnp.dot(tb, jnp.where(m, x, 0.0).astype(BF16), preferred_element_type=F32)
               for tb, x in zip(tbs, xs)]
        ts = [t - jnp.dot(tl.astype(BF16), tb, preferred_element_type=F32)
              for t, tl, tb in zip(ts, tls, tbs)]
        b *= 2
    return ts


def _pair_masks():
    ii = lax.broadcasted_iota(jnp.int32, (PAIR, PAIR), 0)
    jj = lax.broadcasted_iota(jnp.int32, (PAIR, PAIR), 1)
    same_head = (ii ^ jj) < HEAD_DIM
    ti = lax.broadcasted_iota(jnp.int32, (CHUNK, PAIR), 0)
    tl = lax.broadcasted_iota(jnp.int32, (CHUNK, PAIR), 1)
    diag2 = ti == (tl & (HEAD_DIM - 1))
    return ii, jj, same_head & (ii >= jj), same_head & (ii > jj), diag2


def _ffn_body(with_mix, *refs):
    if with_mix:
        (yg_ref, yr_ref, h_ref, wout_ref, mixg_ref, preg_ref, wg_ref, wu_ref, wd_ref, postg_ref,
         o_ref, xn_scr, h_scr, acc_scr) = refs
    else:
        (h_ref, preg_ref, wg_ref, wu_ref, wd_ref, postg_ref, o_ref, xn_scr, h_scr, acc_scr) = refs
    j = pl.program_id(1)

    @pl.when(j == 0)
    def _():
        h = h_ref[...]
        if with_mix:
            y = jnp.concatenate([yg_ref[...], yr_ref[...]], axis=-1)
            mix = _dot(y, wout_ref[...])
            h = h + _rmsnorm(mix, mixg_ref[...])
        h_scr[...] = h
        xn_scr[...] = _rmsnorm(h, preg_ref[...]).astype(BF16)
        acc_scr[...] = jnp.zeros_like(acc_scr)

    xn = xn_scr[...]
    gate = jnp.dot(xn, wg_ref[...], preferred_element_type=F32)
    up = jnp.dot(xn, wu_ref[...], preferred_element_type=F32)
    hid = (_silu(gate) * up).astype(BF16)
    acc_scr[...] += jnp.dot(hid, wd_ref[...], preferred_element_type=F32)

    @pl.when(j == pl.num_programs(1) - 1)
    def _():
        o_ref[...] = h_scr[...] + 0.5 * _rmsnorm(acc_scr[...], postg_ref[...])


def _ffn(h, pre_g, w_gate, w_up, w_down, post_g, mix=None):
    m = h.shape[0]
    grid = (m // FFN_TM, D_FF // FFN_TF)
    row = lambda width: pl.BlockSpec((FFN_TM, width), lambda i, j: (i, 0))
    const = lambda shape: pl.BlockSpec(shape, lambda i, j: (0, 0))
    in_specs, args = [], []
    if mix is not None:
        y_gdn, y_rwkv, w_out, mix_g = mix
        in_specs += [row(D_MIX), row(D_MIX)]
        args += [y_gdn, y_rwkv]
    in_specs.append(row(D_MODEL))
    args.append(h)
    if mix is not None:
        in_specs += [const((D_MODEL, D_MODEL)), const((1, D_MODEL))]
        args += [w_out, mix_g]
    in_specs += [const((1, D_MODEL)),
                 pl.BlockSpec((D_MODEL, FFN_TF), lambda i, j: (0, j)),
                 pl.BlockSpec((D_MODEL, FFN_TF), lambda i, j: (0, j)),
                 pl.BlockSpec((FFN_TF, D_MODEL), lambda i, j: (j, 0)),
                 const((1, D_MODEL))]
    args += [pre_g, w_gate, w_up, w_down, post_g]
    return pl.pallas_call(
        functools.partial(_ffn_body, mix is not None),
        grid=grid,
        in_specs=in_specs,
        out_specs=row(D_MODEL),
        out_shape=jax.ShapeDtypeStruct((m, D_MODEL), F32),
        scratch_shapes=[pltpu.VMEM((FFN_TM, D_MODEL), BF16),
                        pltpu.VMEM((FFN_TM, D_MODEL), F32),
                        pltpu.VMEM((FFN_TM, D_MODEL), F32)],
        compiler_params=pltpu.CompilerParams(
            dimension_semantics=("parallel", "arbitrary"), vmem_limit_bytes=VMEM_LIMIT),
        name="ffn_mix" if mix is not None else "ffn",
    )(*args)


def _proj_body(h_ref, g_ref, w_ref, main_ref, small_ref, rwkv_ref):
    xn = _rmsnorm(h_ref[...], g_ref[...]).astype(BF16)
    p = jnp.dot(xn, w_ref[...], preferred_element_type=F32)
    main_ref[...] = p[:, :GDN_MAIN]
    small_ref[...] = p[:, GDN_MAIN:GDN_MAIN + GDN_SMALL]
    rwkv_ref[...] = p[:, GDN_MAIN + GDN_SMALL:]


def _in_proj(h, g, w):
    m = h.shape[0]
    n = w.shape[1]
    row = lambda width: pl.BlockSpec((PROJ_TM, width), lambda i: (i, 0))
    return pl.pallas_call(
        _proj_body,
        grid=(m // PROJ_TM,),
        in_specs=[row(D_MODEL), pl.BlockSpec((1, D_MODEL), lambda i: (0, 0)),
                  pl.BlockSpec((D_MODEL, n), lambda i: (0, 0))],
        out_specs=[row(GDN_MAIN), row(GDN_SMALL), row(RWKV_COLS)],
        out_shape=[jax.ShapeDtypeStruct((m, GDN_MAIN), F32),
                   jax.ShapeDtypeStruct((m, GDN_SMALL), F32),
                   jax.ShapeDtypeStruct((m, RWKV_COLS), F32)],
        compiler_params=pltpu.CompilerParams(
            dimension_semantics=("parallel",), vmem_limit_bytes=VMEM_LIMIT),
        name="in_proj",
    )(h, g, w)


def _gdn_precompute(loads, masks, wq_s, u_s, attn_s, kdt_s):
    ii, jj, causal, strict, diag2 = masks
    a_list, rest = [], []
    for q, k, v, beta, g_nat, g_col in loads:
        g_last = g_nat[CHUNK - 1:CHUNK, :]
        e_g = jnp.exp(g_nat)
        kb = k * beta
        g_row = jnp.sum(jnp.where(diag2, g_nat, 0.0), axis=0, keepdims=True)
        decay = jnp.exp(jnp.where(causal, g_col - g_row, -jnp.inf))
        gram = _dot_nt(jnp.concatenate([_bd(kb), _bd(q)], axis=0), _bd(k))
        a_list.append(jnp.where(strict, gram[:PAIR] * decay, 0.0))
        attn = (gram[PAIR:] * decay).astype(BF16)
        rhs = jnp.concatenate([_bd(v * beta), _bd(kb * e_g)], axis=1)
        kdt = _bd(k * jnp.exp(g_last - g_nat), F32).T.astype(BF16)
        rest.append((attn, rhs, _bd(q * e_g), kdt))
    ts = _tri_inverse_many(a_list, ii, jj)
    for n, (t, (attn, rhs, qg, kdt)) in enumerate(zip(ts, rest)):
        uw = jnp.dot(t.astype(BF16), rhs, preferred_element_type=F32)
        wq_s[n] = jnp.concatenate([uw[:, PAIR:].astype(BF16), qg], axis=0)
        u_s[n] = uw[:, :PAIR]
        attn_s[n] = attn
        kdt_s[n] = kdt


def _gdn_body(xm_ref, xs_ref, convw_ref, alog_ref, dtb_ref, ng_ref, lblk_ref, e512_ref, eb_ref,
              eg_ref, egall_ref, o_ref, xbuf, q_s, k_s, v_s, beta_s, g_s, gall_s, o_s, state,
              wq_s, u_s, attn_s, kdt_s):
    tb = MIX_TB

    @pl.when(pl.program_id(1) == 0)
    def _():
        xbuf[0:SUBLANES, :] = jnp.zeros((SUBLANES, 3 * D_MIX), F32)
        state[...] = jnp.zeros_like(state)

    xbuf[SUBLANES:SUBLANES + tb, :] = xm_ref[0, :, 0:3 * D_MIX]
    conv = None
    for j in range(CONV_WIDTH):
        term = xbuf[pl.ds(SUBLANES - (CONV_WIDTH - 1) + j, tb), :] * convw_ref[j:j + 1, :]
        conv = term if conv is None else conv + term
    act = _silu(conv)
    e512 = e512_ref[...]
    q = act[:, 0:D_MIX]
    k = act[:, D_MIX:2 * D_MIX]
    q_s[...] = q * lax.rsqrt(_dot_sel(q * q, e512, 2) + L2_EPS) * (HEAD_DIM ** -0.5)
    k_s[...] = k * lax.rsqrt(_dot_sel(k * k, e512, 2) + L2_EPS)
    v_s[...] = act[:, 2 * D_MIX:3 * D_MIX]

    xs = xs_ref[0]
    beta_s[...] = _dot_sel(_sigmoid(xs), eb_ref[...])
    g_small = -jnp.exp(alog_ref[...]) * _softplus(xs + dtb_ref[...])
    g_cum = _sel_dot(lblk_ref[...], g_small)
    g_s[...] = _dot_sel(g_cum, eg_ref[...])
    gall_s[...] = _dot_sel(g_cum, egall_ref[...])

    masks = _pair_masks()
    n_chunks = tb // CHUNK

    def g_col_of(c, p):
        rows = pl.ds(c * CHUNK, CHUNK)
        return jnp.concatenate([gall_s[rows, pl.ds(2 * p * LANES, LANES)],
                                gall_s[rows, pl.ds((2 * p + 1) * LANES, LANES)]], axis=0)

    loads = []
    for c in range(n_chunks):
        rows = pl.ds(c * CHUNK, CHUNK)
        for p in range(N_PAIRS):
            cols = pl.ds(p * PAIR, PAIR)
            loads.append((q_s[rows, cols], k_s[rows, cols], v_s[rows, cols], beta_s[rows, cols],
                          g_s[rows, cols], g_col_of(c, p)))
    _gdn_precompute(loads, masks, wq_s, u_s, attn_s, kdt_s)

    s = [state[p] for p in range(N_PAIRS)]
    for c in range(n_chunks):
        rows = pl.ds(c * CHUNK, CHUNK)
        ws_qs = [jnp.dot(wq_s[c * N_PAIRS + p], s[p].astype(BF16), preferred_element_type=F32)
                 for p in range(N_PAIRS)]
        v_new = [(u_s[c * N_PAIRS + p] - ws_qs[p][:PAIR]).astype(BF16) for p in range(N_PAIRS)]
        for p in range(N_PAIRS):
            g_col = g_col_of(c, p)
            g_last_col = jnp.concatenate(
                [jnp.broadcast_to(g_col[CHUNK - 1:CHUNK, :], (CHUNK, PAIR)),
                 jnp.broadcast_to(g_col[PAIR - 1:PAIR, :], (CHUNK, PAIR))], axis=0)
            s[p] = s[p] * jnp.exp(g_last_col) + jnp.dot(kdt_s[c * N_PAIRS + p], v_new[p],
                                                         preferred_element_type=F32)
        for p in range(N_PAIRS):
            o = ws_qs[p][PAIR:] + jnp.dot(attn_s[c * N_PAIRS + p], v_new[p], preferred_element_type=F32)
            o_s[rows, pl.ds(p * PAIR, PAIR)] = _unbd(o)
    for p in range(N_PAIRS):
        state[p] = s[p]

    o = o_s[...]
    z = xm_ref[0, :, 3 * D_MIX:4 * D_MIX]
    ms = _dot_sel(o * o, e512, 2) * (1.0 / HEAD_DIM)
    o_ref[0] = o * lax.rsqrt(ms + EPS) * ng_ref[...] * _silu(z)
    xbuf[0:SUBLANES, :] = xbuf[tb:tb + SUBLANES, :]


def _gdn(x_main, x_small, conv_w, a_log_pad, dt_bias_pad, norm_g_full, consts):
    b, t, _ = x_main.shape
    tb = MIX_TB
    lblk, e512, eb, eg, egall = consts
    n_prob = (tb // CHUNK) * N_PAIRS
    const = lambda arr: pl.BlockSpec(arr.shape, lambda i, j: (0,) * arr.ndim)
    smalls = [conv_w, a_log_pad, dt_bias_pad, norm_g_full, lblk, e512, eb, eg, egall]
    return pl.pallas_call(
        _gdn_body,
        grid=(b, t // tb),
        in_specs=[pl.BlockSpec((1, tb, GDN_MAIN), lambda i, j: (i, j, 0)),
                  pl.BlockSpec((1, tb, GDN_SMALL), lambda i, j: (i, j, 0))] + [const(a) for a in smalls],
        out_specs=pl.BlockSpec((1, tb, D_MIX), lambda i, j: (i, j, 0)),
        out_shape=jax.ShapeDtypeStruct((b, t, D_MIX), F32),
        scratch_shapes=[pltpu.VMEM((tb + SUBLANES, 3 * D_MIX), F32)]
        + [pltpu.VMEM((tb, D_MIX), F32)] * 5
        + [pltpu.VMEM((tb, N_HEADS * LANES), F32),
           pltpu.VMEM((tb, D_MIX), F32),
           pltpu.VMEM((N_PAIRS, PAIR, PAIR), F32),
           pltpu.VMEM((n_prob, 2 * PAIR, PAIR), BF16),
           pltpu.VMEM((n_prob, PAIR, PAIR), F32),
           pltpu.VMEM((n_prob, PAIR, PAIR), BF16),
           pltpu.VMEM((n_prob, PAIR, PAIR), BF16)],
        compiler_params=pltpu.CompilerParams(
            dimension_semantics=("parallel", "arbitrary"), vmem_limit_bytes=VMEM_LIMIT),
        name="gdn",
    )(x_main, x_small, *smalls)


def _rwkv_precompute(loads, masks, wr_s, ut_s, arb_s, y0_s, btt_s, z0_s, pc_s):
    ii, jj, causal, strict, _ = masks
    x_list, rest = [], []
    for r, k, v, aa, bb, lw, lp in loads:
        p_inv = jnp.exp(-lp)
        lp_last = lp[CHUNK - 1:CHUNK, :]
        p_rest = jnp.exp(lp_last - lp)
        a_h = _bd(aa * jnp.exp(lp - lw))
        r_h = _bd(r * jnp.exp(lp))
        v_bd = _bd(v)
        gram = _dot_nt(jnp.concatenate([a_h, r_h], axis=0),
                       jnp.concatenate([_bd(bb * p_inv), _bd(k * p_inv)], axis=0))
        x_list.append(jnp.where(strict, -gram[:PAIR, :PAIR], 0.0))
        a_ak = jnp.where(strict, gram[:PAIR, PAIR:], 0.0)
        a_rb = jnp.where(causal, gram[PAIR:, :PAIR], 0.0).astype(BF16)
        a_rk = jnp.where(causal, gram[PAIR:, PAIR:], 0.0)
        av = _dot(jnp.concatenate([a_ak, a_rk], axis=0), v_bd)
        btt = _bd(bb * p_rest, F32).T.astype(BF16)
        z0 = _dot_tn(_bd(k * p_rest), v_bd)
        pc = jnp.broadcast_to(jnp.exp(lp_last), (PAIR, PAIR)).T
        rest.append((a_h, r_h, a_rb, av, btt, z0, pc))
    ts = _tri_inverse_many(x_list, ii, jj)
    for n, (t, (a_h, r_h, a_rb, av, btt, z0, pc)) in enumerate(zip(ts, rest)):
        wu = jnp.dot(t.astype(BF16), jnp.concatenate([a_h, av[:PAIR].astype(BF16)], axis=1),
                     preferred_element_type=F32)
        wr_s[n] = jnp.concatenate([wu[:, :PAIR].astype(BF16), r_h], axis=0)
        ut_s[n] = wu[:, PAIR:]
        arb_s[n] = a_rb
        y0_s[n] = av[PAIR:]
        btt_s[n] = btt
        z0_s[n] = z0
        pc_s[n] = pc


def _rwkv_body(p_ref, mu_ref, w0_ref, a0_ref, w2_ref, a2_ref, g2_ref, kk_ref, ka_ref, rk_ref,
               lng_ref, lnb_ref, lblk_ref, e512_ref, o_ref,
               pbuf, r_s, k_s, v_s, aa_s, bb_s, lw_s, lp_s, gate_s, y_s, state,
               wr_s, ut_s, arb_s, y0_s, btt_s, z0_s, pc_s):
    tb = MIX_TB

    @pl.when(pl.program_id(1) == 0)
    def _():
        pbuf[0:SUBLANES, :] = jnp.zeros((SUBLANES, RWKV_COLS), F32)
        state[...] = jnp.zeros_like(state)

    pbuf[SUBLANES:SUBLANES + tb, :] = p_ref[0]
    cur = pbuf[SUBLANES:SUBLANES + tb, :]
    prev = pbuf[pl.ds(SUBLANES - 1, tb), :]
    p = cur + (prev - cur) * mu_ref[...]
    r = p[:, 0:D_MIX]
    k = p[:, D_MIX:2 * D_MIX]
    v = p[:, 2 * D_MIX:3 * D_MIX]
    lora_wa = p[:, 3 * D_MIX:3 * D_MIX + LORA_W + LORA_A]
    g_lo = p[:, 3 * D_MIX + LORA_W + LORA_A:]
    e512 = e512_ref[...]

    w = -_softplus(-(w0_ref[...] + _dot(jnp.tanh(lora_wa), w2_ref[...]))) - 0.5
    lw = -jnp.exp(w)
    a = _sigmoid(a0_ref[...] + _dot(lora_wa, a2_ref[...]))
    gate_s[...] = _dot(_sigmoid(g_lo), g2_ref[...])
    kk = k * kk_ref[...]
    kk = kk * lax.rsqrt(_dot_sel(kk * kk, e512, 2) + L2_EPS)
    k = k * (1.0 + (a - 1.0) * ka_ref[...])
    r_s[...] = r
    k_s[...] = k
    v_s[...] = v
    aa_s[...] = -kk
    bb_s[...] = kk * a
    lw_s[...] = lw
    lp_s[...] = _sel_dot(lblk_ref[...], lw)

    masks = _pair_masks()
    n_chunks = tb // CHUNK
    loads = []
    for c in range(n_chunks):
        rows = pl.ds(c * CHUNK, CHUNK)
        for pr in range(N_PAIRS):
            cols = pl.ds(pr * PAIR, PAIR)
            loads.append((r_s[rows, cols], k_s[rows, cols], v_s[rows, cols], aa_s[rows, cols],
                          bb_s[rows, cols], lw_s[rows, cols], lp_s[rows, cols]))
    _rwkv_precompute(loads, masks, wr_s, ut_s, arb_s, y0_s, btt_s, z0_s, pc_s)

    hs = [state[pr] for pr in range(N_PAIRS)]
    for c in range(n_chunks):
        rows = pl.ds(c * CHUNK, CHUNK)
        wh_rh = [jnp.dot(wr_s[c * N_PAIRS + pr], hs[pr].astype(BF16), preferred_element_type=F32)
                 for pr in range(N_PAIRS)]
        u = [(wh_rh[pr][:PAIR] + ut_s[c * N_PAIRS + pr]).astype(BF16) for pr in range(N_PAIRS)]
        for pr in range(N_PAIRS):
            n = c * N_PAIRS + pr
            hs[pr] = (pc_s[n] * hs[pr] + jnp.dot(btt_s[n], u[pr], preferred_element_type=F32)
                      + z0_s[n])
        for pr in range(N_PAIRS):
            n = c * N_PAIRS + pr
            y = wh_rh[pr][PAIR:] + jnp.dot(arb_s[n], u[pr], preferred_element_type=F32) + y0_s[n]
            y_s[rows, pl.ds(pr * PAIR, PAIR)] = _unbd(y)
    for pr in range(N_PAIRS):
        state[pr] = hs[pr]

    y = y_s[...]
    inv_d = 1.0 / HEAD_DIM
    yc = y - _dot_sel(y, e512, 2) * inv_d
    y = yc * lax.rsqrt(_dot_sel(yc * yc, e512, 2) * inv_d + GN_EPS) * lng_ref[...] + lnb_ref[...]
    bonus = _dot_sel(r_s[...] * k_s[...] * rk_ref[...], e512, 2) * v_s[...]
    o_ref[0] = (y + bonus) * gate_s[...]
    pbuf[0:SUBLANES, :] = pbuf[tb:tb + SUBLANES, :]


def _rwkv(p, params, consts):
    b, t, _ = p.shape
    tb = MIX_TB
    lblk, e512 = consts
    n_prob = (tb // CHUNK) * N_PAIRS
    const = lambda arr: pl.BlockSpec(arr.shape, lambda i, j: (0,) * arr.ndim)
    smalls = list(params) + [lblk, e512]
    return pl.pallas_call(
        _rwkv_body,
        grid=(b, t // tb),
        in_specs=[pl.BlockSpec((1, tb, RWKV_COLS), lambda i, j: (i, j, 0))] + [const(a) for a in smalls],
        out_specs=pl.BlockSpec((1, tb, D_MIX), lambda i, j: (i, j, 0)),
        out_shape=jax.ShapeDtypeStruct((b, t, D_MIX), F32),
        scratch_shapes=[pltpu.VMEM((tb + SUBLANES, RWKV_COLS), F32)]
        + [pltpu.VMEM((tb, D_MIX), F32)] * 9
        + [pltpu.VMEM((N_PAIRS, PAIR, PAIR), F32),
           pltpu.VMEM((n_prob, 2 * PAIR, PAIR), BF16),
           pltpu.VMEM((n_prob, PAIR, PAIR), F32),
           pltpu.VMEM((n_prob, PAIR, PAIR), BF16),
           pltpu.VMEM((n_prob, PAIR, PAIR), F32),
           pltpu.VMEM((n_prob, PAIR, PAIR), BF16),
           pltpu.VMEM((n_prob, PAIR, PAIR), F32),
           pltpu.VMEM((n_prob, PAIR, PAIR), F32)],
        compiler_params=pltpu.CompilerParams(
            dimension_semantics=("parallel", "arbitrary"), vmem_limit_bytes=VMEM_LIMIT),
        name="rwkv",
    )(p, *smalls)


def _selection_constants(tb):
    i = jnp.arange(tb)
    lblk = ((i[:, None] // CHUNK == i[None, :] // CHUNK) & (i[:, None] >= i[None, :])).astype(BF16)
    c = jnp.arange(D_MIX)
    e512 = (c[:, None] // HEAD_DIM == c[None, :] // HEAD_DIM).astype(BF16)
    row = jnp.arange(GDN_SMALL)
    eb = (row[:, None] == c[None, :] // HEAD_DIM).astype(BF16)
    eg = (row[:, None] == N_HEADS + c[None, :] // HEAD_DIM).astype(BF16)
    ca = jnp.arange(N_HEADS * LANES)
    egall = (row[:, None] == N_HEADS + ca[None, :] // LANES).astype(BF16)
    return lblk, e512, eb, eg, egall


def kernel(x, ffn1_pre_g, ffn1_w_gate, ffn1_w_up, ffn1_w_down, ffn1_post_g, mix_pre_g, w_in, gdn_conv_w, gdn_a_log, gdn_dt_bias, gdn_norm_g, rwkv_mu, rwkv_w0, rwkv_w2, rwkv_a0, rwkv_a2, rwkv_g2, rwkv_k_k, rwkv_k_a, rwkv_r_k, rwkv_ln_g, rwkv_ln_b, w_out, mix_post_g, ffn2_pre_g, ffn2_w_gate, ffn2_w_up, ffn2_w_down, ffn2_post_g):
    b, t, d = x.shape
    depth = ffn1_pre_g.shape[0]
    lblk, e512, eb, eg, egall = _selection_constants(MIX_TB)
    row = lambda v: v.reshape(1, -1).astype(F32)
    h = x.reshape(b * t, d)
    for l in range(depth):
        h = _ffn(h, row(ffn1_pre_g[l]), ffn1_w_gate[l].astype(BF16), ffn1_w_up[l].astype(BF16),
                 ffn1_w_down[l].astype(BF16), row(ffn1_post_g[l]))

        wl = w_in[l]
        n_qkvz = 4 * D_MIX
        w_cat = jnp.concatenate(
            [wl[:, :n_qkvz], wl[:, n_qkvz:n_qkvz + 2 * N_HEADS],
             jnp.zeros((d, GDN_SMALL - 2 * N_HEADS), wl.dtype), wl[:, n_qkvz + 2 * N_HEADS:]], axis=1)
        x_main, x_small, x_rwkv = _in_proj(h, row(mix_pre_g[l]), w_cat.astype(BF16))

        pad_small = lambda v: jnp.zeros((1, GDN_SMALL), F32).at[0, N_HEADS:2 * N_HEADS].set(v)
        y_gdn = _gdn(x_main.reshape(b, t, GDN_MAIN), x_small.reshape(b, t, GDN_SMALL),
                     gdn_conv_w[l].astype(F32), pad_small(gdn_a_log[l]), pad_small(gdn_dt_bias[l]),
                     row(jnp.tile(gdn_norm_g[l], N_HEADS)), (lblk, e512, eb, eg, egall))

        zeros_lora = jnp.zeros((LORA_W, D_MIX), F32)
        w2_pad = jnp.concatenate([rwkv_w2[l], zeros_lora], axis=0).astype(BF16)
        a2_pad = jnp.concatenate([zeros_lora, rwkv_a2[l]], axis=0).astype(BF16)
        rwkv_params = (row(rwkv_mu[l]), row(rwkv_w0[l]), row(rwkv_a0[l]), w2_pad, a2_pad,
                       rwkv_g2[l].astype(BF16), row(rwkv_k_k[l]), row(rwkv_k_a[l]), row(rwkv_r_k[l]),
                       row(rwkv_ln_g[l]), row(rwkv_ln_b[l]))
        y_rwkv = _rwkv(x_rwkv.reshape(b, t, RWKV_COLS), rwkv_params, (lblk, e512))

        h = _ffn(h, row(ffn2_pre_g[l]), ffn2_w_gate[l].astype(BF16), ffn2_w_up[l].astype(BF16),
                 ffn2_w_down[l].astype(BF16), row(ffn2_post_g[l]),
                 mix=(y_gdn.reshape(b * t, D_MIX), y_rwkv.reshape(b * t, D_MIX),
                      w_out[l].astype(BF16), row(mix_post_g[l])))
    return h.reshape(b, t, d)
```

```python
import functools

import jax
import jax.numpy as jnp
from jax import lax
from jax.experimental import pallas as pl
from jax.experimental.pallas import tpu as pltpu

F32 = jnp.float32
BF16 = jnp.bfloat16

D_MODEL = 1024
D_FF = 2816
HEAD_DIM = 64
N_HEADS = 8
D_MIX = N_HEADS * HEAD_DIM
N_PAIRS = N_HEADS // 2
PAIR = 2 * HEAD_DIM
CHUNK = 64
CONV_WIDTH = 4
LORA_W = 64
LORA_A = 64
LORA_G = 128
EPS = 1e-6
L2_EPS = 1e-6
GN_EPS = 64e-5

LANES = 128
SUBLANES = 8
GDN_MAIN = 4 * D_MIX
GDN_SMALL = LANES
RWKV_COLS = 3 * D_MIX + LORA_W + LORA_A + LORA_G

FFN_TM = 512
FFN_TF = 1408
PROJ_TM = 512
MIX_TB = 256
VMEM_LIMIT = 56 * 1024 * 1024


def _dot(a, b):
    return jnp.dot(a.astype(BF16), b.astype(BF16), preferred_element_type=F32)


def _dot_nt(a, b):
    return lax.dot_general(a.astype(BF16), b.astype(BF16), (((1,), (1,)), ((), ())),
                           preferred_element_type=F32)


def _dot_tn(a, b):
    return lax.dot_general(a.astype(BF16), b.astype(BF16), (((0,), (0,)), ((), ())),
                           preferred_element_type=F32)


def _split(x, passes):
    pieces = []
    rem = x
    for i in range(passes):
        p = rem.astype(BF16)
        pieces.append(p)
        if i + 1 < passes:
            rem = rem - p.astype(F32)
    return pieces


def _head_sums(x, e128):
    xb = x.astype(BF16)
    return jnp.concatenate(
        [jnp.dot(xb[:, g * PAIR:(g + 1) * PAIR], e128, preferred_element_type=F32)
         for g in range(N_PAIRS)], axis=1)


def _sel_dot(sel, x, passes=3):
    out = None
    for p in _split(x, passes):
        t = jnp.dot(sel, p, preferred_element_type=F32)
        out = t if out is None else out + t
    return out


def _rmsnorm(x, g):
    return x * lax.rsqrt(jnp.mean(x * x, axis=-1, keepdims=True) + EPS) * g


def _sigmoid(x):
    return jax.nn.sigmoid(x)


def _silu(x):
    hx = 0.5 * x
    return hx + hx * jnp.tanh(hx)


def _softplus(x):
    return jnp.maximum(x, 0.0) + jnp.log1p(jnp.exp(-jnp.abs(x)))


def _bd(x):
    x = x.astype(BF16)
    lane = lax.broadcasted_iota(jnp.int32, x.shape, 1)
    zero = jnp.zeros_like(x)
    return jnp.concatenate([jnp.where(lane < HEAD_DIM, x, zero),
                            jnp.where(lane >= HEAD_DIM, x, zero)], axis=0)


def _tri_inverse_many(xs, ti, tj):
    def sub_blocks(b):
        return ((ti ^ tj) < 2 * b) & ((ti & b) != 0) & ((tj & b) == 0)

    eye = jnp.where(ti == tj, 1.0, 0.0)
    ts = [(eye - jnp.where(sub_blocks(1), x, 0.0)).astype(BF16) for x in xs]
    xbs = [x.astype(BF16) for x in xs]
    zero = jnp.zeros((CHUNK, PAIR), BF16)
    b = 2
    while b < CHUNK:
        m = sub_blocks(b)
        tls = [jnp.dot(t, _bd(jnp.where(m, xb, zero)), preferred_element_type=F32)
               for t, xb in zip(ts, xbs)]
        ts = [jnp.dot((eye - tl).astype(BF16), _bd(t), preferred_element_type=F32).astype(BF16)
              for tl, t in zip(tls, ts)]
        b *= 2
    return ts


def _pair_masks():
    ti = lax.broadcasted_iota(jnp.int32, (CHUNK, PAIR), 0)
    tj = lax.broadcasted_iota(jnp.int32, (CHUNK, PAIR), 1) & (HEAD_DIM - 1)
    ii = lax.broadcasted_iota(jnp.int32, (PAIR, PAIR), 0)
    jj = lax.broadcasted_iota(jnp.int32, (PAIR, PAIR), 1)
    return ti, tj, ti >= tj, ti > tj, (ii ^ jj) < HEAD_DIM


def _ffn_body(with_mix, *refs):
    if with_mix:
        (yg_ref, yr_ref, h_ref, wout_ref, mixg_ref, preg_ref, wg_ref, wu_ref, wd_ref, postg_ref,
         o_ref, xn_scr, h_scr, acc_scr) = refs
    else:
        (h_ref, preg_ref, wg_ref, wu_ref, wd_ref, postg_ref, o_ref, xn_scr, h_scr, acc_scr) = refs
    j = pl.program_id(1)

    @pl.when(j == 0)
    def _():
        h = h_ref[...]
        if with_mix:
            y = jnp.concatenate([yg_ref[...], yr_ref[...]], axis=-1)
            mix = _dot(y, wout_ref[...])
            h = h + _rmsnorm(mix, mixg_ref[...])
        h_scr[...] = h
        xn_scr[...] = _rmsnorm(h, preg_ref[...]).astype(BF16)
        acc_scr[...] = jnp.zeros_like(acc_scr)

    xn = xn_scr[...]
    gate = jnp.dot(xn, wg_ref[...], preferred_element_type=F32)
    up = jnp.dot(xn, wu_ref[...], preferred_element_type=F32)
    hid = (_silu(gate) * up).astype(BF16)
    acc_scr[...] += jnp.dot(hid, wd_ref[...], preferred_element_type=F32)

    @pl.when(j == pl.num_programs(1) - 1)
    def _():
        o_ref[...] = h_scr[...] + 0.5 * _rmsnorm(acc_scr[...], postg_ref[...])


def _ffn(h, pre_g, w_gate, w_up, w_down, post_g, mix=None):
    m = h.shape[0]
    grid = (m // FFN_TM, D_FF // FFN_TF)
    row = lambda width: pl.BlockSpec((FFN_TM, width), lambda i, j: (i, 0))
    const = lambda shape: pl.BlockSpec(shape, lambda i, j: (0, 0))
    in_specs, args = [], []
    if mix is not None:
        y_gdn, y_rwkv, w_out, mix_g = mix
        in_specs += [row(D_MIX), row(D_MIX)]
        args += [y_gdn, y_rwkv]
    in_specs.append(row(D_MODEL))
    args.append(h)
    if mix is not None:
        in_specs += [const((D_MODEL, D_MODEL)), const((1, D_MODEL))]
        args += [w_out, mix_g]
    in_specs += [const((1, D_MODEL)),
                 pl.BlockSpec((D_MODEL, FFN_TF), lambda i, j: (0, j)),
                 pl.BlockSpec((D_MODEL, FFN_TF), lambda i, j: (0, j)),
                 pl.BlockSpec((FFN_TF, D_MODEL), lambda i, j: (j, 0)),
                 const((1, D_MODEL))]
    args += [pre_g, w_gate, w_up, w_down, post_g]
    return pl.pallas_call(
        functools.partial(_ffn_body, mix is not None),
        grid=grid,
        in_specs=in_specs,
        out_specs=row(D_MODEL),
        out_shape=jax.ShapeDtypeStruct((m, D_MODEL), F32),
        scratch_shapes=[pltpu.VMEM((FFN_TM, D_MODEL), BF16),
                        pltpu.VMEM((FFN_TM, D_MODEL), F32),
                        pltpu.VMEM((FFN_TM, D_MODEL), F32)],
        compiler_params=pltpu.CompilerParams(
            dimension_semantics=("parallel", "arbitrary"), vmem_limit_bytes=VMEM_LIMIT),
        name="ffn_mix" if mix is not None else "ffn",
    )(*args)


def _proj_body(tiles_per_seq, h_ref, g_ref, w_ref, convw_ref, mu_ref, main_ref, small_ref, rwkv_ref,
               qkv_buf, rwkv_buf):
    tm = PROJ_TM

    @pl.when(pl.program_id(0) % tiles_per_seq == 0)
    def _():
        qkv_buf[0:SUBLANES, :] = jnp.zeros((SUBLANES, 3 * D_MIX), F32)
        rwkv_buf[0:SUBLANES, :] = jnp.zeros((SUBLANES, RWKV_COLS), F32)

    xn = _rmsnorm(h_ref[...], g_ref[...]).astype(BF16)
    proj = lambda lo, hi: jnp.dot(xn, w_ref[:, lo:hi], preferred_element_type=F32)

    rwkv_buf[SUBLANES:SUBLANES + tm, :] = proj(GDN_MAIN + GDN_SMALL, GDN_MAIN + GDN_SMALL + RWKV_COLS)
    pall = rwkv_buf[...]
    cur = pall[SUBLANES:]
    prev = pltpu.roll(pall, 1, axis=0)[SUBLANES:]
    rwkv_ref[...] = cur + (prev - cur) * mu_ref[...]

    qkv_buf[SUBLANES:SUBLANES + tm, :] = proj(0, 3 * D_MIX)
    xall = qkv_buf[...]
    conv = xall[SUBLANES:] * convw_ref[CONV_WIDTH - 1:CONV_WIDTH, :]
    for d in range(1, CONV_WIDTH):
        shifted = pltpu.roll(xall, d, axis=0)[SUBLANES:]
        conv = conv + shifted * convw_ref[CONV_WIDTH - 1 - d:CONV_WIDTH - d, :]
    main_ref[:, 0:3 * D_MIX] = _silu(conv)

    main_ref[:, 3 * D_MIX:] = proj(3 * D_MIX, GDN_MAIN)
    small_ref[...] = proj(GDN_MAIN, GDN_MAIN + GDN_SMALL)

    qkv_buf[0:SUBLANES, :] = qkv_buf[tm:tm + SUBLANES, :]
    rwkv_buf[0:SUBLANES, :] = rwkv_buf[tm:tm + SUBLANES, :]


def _in_proj(h, g, w, conv_w, mu, seq_len):
    m = h.shape[0]
    n = w.shape[1]
    row = lambda width: pl.BlockSpec((PROJ_TM, width), lambda i: (i, 0))
    const = lambda arr: pl.BlockSpec(arr.shape, lambda i: (0, 0))
    return pl.pallas_call(
        functools.partial(_proj_body, seq_len // PROJ_TM),
        grid=(m // PROJ_TM,),
        in_specs=[row(D_MODEL), const(g), pl.BlockSpec((D_MODEL, n), lambda i: (0, 0)),
                  const(conv_w), const(mu)],
        out_specs=[row(GDN_MAIN), row(GDN_SMALL), row(RWKV_COLS)],
        out_shape=[jax.ShapeDtypeStruct((m, GDN_MAIN), F32),
                   jax.ShapeDtypeStruct((m, GDN_SMALL), F32),
                   jax.ShapeDtypeStruct((m, RWKV_COLS), F32)],
        scratch_shapes=[pltpu.VMEM((PROJ_TM + SUBLANES, 3 * D_MIX), F32),
                        pltpu.VMEM((PROJ_TM + SUBLANES, RWKV_COLS), F32)],
        compiler_params=pltpu.CompilerParams(
            dimension_semantics=("arbitrary",), vmem_limit_bytes=VMEM_LIMIT),
        name="in_proj",
    )(h, g, w, conv_w, mu)


def _gdn_precompute(loads, masks, wq_s, u_s, attn_s, kdt_s):
    ti, tj, causal, strict, _ = masks
    a_list, rest = [], []
    for q, k, v, beta, g_nat in loads:
        g_last = g_nat[CHUNK - 1:CHUNK, :]
        e_g = jnp.exp(g_nat)
        kb = k * beta
        g_row = jnp.sum(jnp.where(ti == tj, g_nat, 0.0), axis=0, keepdims=True)
        decay = jnp.exp(jnp.where(causal, g_nat - g_row, -jnp.inf))
        gram = _dot_nt(jnp.concatenate([kb, q], axis=0), _bd(k))
        a_list.append(jnp.where(strict, gram[:CHUNK] * decay, 0.0))
        attn = (gram[CHUNK:] * decay).astype(BF16)
        rhs = jnp.concatenate([_bd(v * beta), _bd(kb * e_g)], axis=1)
        kdt = (k * jnp.exp(g_last - g_nat)).T.astype(BF16)
        rest.append((attn, rhs, (q * e_g).astype(BF16), kdt))
    ts = _tri_inverse_many(a_list, ti, tj)
    for n, (t, (attn, rhs, qg, kdt)) in enumerate(zip(ts, rest)):
        uw = jnp.dot(t, rhs, preferred_element_type=F32)
        wq_s[n] = jnp.concatenate([uw[:, PAIR:].astype(BF16), qg], axis=0)
        u_s[n] = uw[:, :PAIR]
        attn_s[n] = attn
        kdt_s[n] = kdt


def _gdn_body(xm_ref, xs_ref, alog_ref, dtb_ref, ng_ref, lblk_ref, e128_ref,
              o_ref, q_s, k_s, beta_s, g_s, o_s, state,
              wq_s, u_s, attn_s, kdt_s):
    tb = MIX_TB

    @pl.when(pl.program_id(1) == 0)
    def _():
        state[...] = jnp.zeros_like(state)

    e128 = e128_ref[...]
    q = xm_ref[0, :, 0:D_MIX]
    k = xm_ref[0, :, D_MIX:2 * D_MIX]
    q_s[...] = q * lax.rsqrt(_head_sums(q * q, e128) + L2_EPS) * (HEAD_DIM ** -0.5)
    k_s[...] = k * lax.rsqrt(_head_sums(k * k, e128) + L2_EPS)

    xs = xs_ref[0]
    beta = _sigmoid(xs)
    g_small = -jnp.exp(alog_ref[...]) * _softplus(xs + dtb_ref[...])
    g_cum = _sel_dot(lblk_ref[...], g_small)
    first_head = lax.broadcasted_iota(jnp.int32, (tb, LANES), 1) < HEAD_DIM
    lane_bcast = lambda x, j: jnp.broadcast_to(x[:, j:j + 1], (tb, LANES))
    for p in range(N_PAIRS):
        cols = pl.ds(p * PAIR, PAIR)
        g_s[:, cols] = jnp.where(first_head, lane_bcast(g_cum, N_HEADS + 2 * p),
                                 lane_bcast(g_cum, N_HEADS + 2 * p + 1))
        beta_s[:, cols] = jnp.where(first_head, lane_bcast(beta, 2 * p), lane_bcast(beta, 2 * p + 1))

    masks = _pair_masks()
    same_head = masks[-1]
    n_chunks = tb // CHUNK
    loads = []
    for c in range(n_chunks):
        rows = pl.ds(c * CHUNK, CHUNK)
        for p in range(N_PAIRS):
            cols = pl.ds(p * PAIR, PAIR)
            v = xm_ref[0, rows, pl.ds(2 * D_MIX + p * PAIR, PAIR)]
            loads.append((q_s[rows, cols], k_s[rows, cols], v, beta_s[rows, cols], g_s[rows, cols]))
    _gdn_precompute(loads, masks, wq_s, u_s, attn_s, kdt_s)

    s = [state[p] for p in range(N_PAIRS)]
    for c in range(n_chunks):
        rows = pl.ds(c * CHUNK, CHUNK)
        last = c * CHUNK + CHUNK - 1
        ws_qs = [jnp.dot(wq_s[c * N_PAIRS + p], s[p].astype(BF16), preferred_element_type=F32)
                 for p in range(N_PAIRS)]
        v_new = [(u_s[c * N_PAIRS + p] - ws_qs[p][:CHUNK]).astype(BF16) for p in range(N_PAIRS)]
        for p in range(N_PAIRS):
            head_decay = lambda h: jnp.broadcast_to(
                jnp.exp(g_cum[last:last + 1, N_HEADS + h:N_HEADS + h + 1]), (HEAD_DIM, PAIR))
            s_decay = jnp.concatenate([head_decay(2 * p), head_decay(2 * p + 1)], axis=0)
            kv = jnp.dot(kdt_s[c * N_PAIRS + p], v_new[p], preferred_element_type=F32)
            s[p] = s[p] * s_decay + jnp.where(same_head, kv, 0.0)
        for p in range(N_PAIRS):
            o_s[rows, pl.ds(p * PAIR, PAIR)] = ws_qs[p][CHUNK:] + jnp.dot(
                attn_s[c * N_PAIRS + p], _bd(v_new[p]), preferred_element_type=F32)
    for p in range(N_PAIRS):
        state[p] = s[p]

    o = o_s[...]
    z = xm_ref[0, :, 3 * D_MIX:4 * D_MIX]
    ms = _head_sums(o * o, e128) * (1.0 / HEAD_DIM)
    o_ref[0] = o * lax.rsqrt(ms + EPS) * ng_ref[...] * _silu(z)


def _gdn(x_main, x_small, a_log_pad, dt_bias_pad, norm_g_full, consts):
    b, t, _ = x_main.shape
    tb = MIX_TB
    lblk, e128 = consts
    n_prob = (tb // CHUNK) * N_PAIRS
    const = lambda arr: pl.BlockSpec(arr.shape, lambda i, j: (0,) * arr.ndim)
    smalls = [a_log_pad, dt_bias_pad, norm_g_full, lblk, e128]
    return pl.pallas_call(
        _gdn_body,
        grid=(b, t // tb),
        in_specs=[pl.BlockSpec((1, tb, GDN_MAIN), lambda i, j: (i, j, 0)),
                  pl.BlockSpec((1, tb, GDN_SMALL), lambda i, j: (i, j, 0))] + [const(a) for a in smalls],
        out_specs=pl.BlockSpec((1, tb, D_MIX), lambda i, j: (i, j, 0)),
        out_shape=jax.ShapeDtypeStruct((b, t, D_MIX), F32),
        scratch_shapes=[pltpu.VMEM((tb, D_MIX), F32)] * 5
        + [pltpu.VMEM((N_PAIRS, PAIR, PAIR), F32),
           pltpu.VMEM((n_prob, 2 * CHUNK, PAIR), BF16),
           pltpu.VMEM((n_prob, CHUNK, PAIR), F32),
           pltpu.VMEM((n_prob, CHUNK, PAIR), BF16),
           pltpu.VMEM((n_prob, PAIR, CHUNK), BF16)],
        compiler_params=pltpu.CompilerParams(
            dimension_semantics=("parallel", "arbitrary"), vmem_limit_bytes=VMEM_LIMIT),
        name="gdn",
    )(x_main, x_small, *smalls)


def _rwkv_precompute(loads, masks, wr_s, ut_s, arb_s, y0_s, btt_s, z0_s, pc_s):
    ti, tj, causal, strict, same_head = masks
    x_list, rest = [], []
    for r, k, v, aa, bb, lw, lp in loads:
        p_inv = jnp.exp(-lp)
        lp_last = lp[CHUNK - 1:CHUNK, :]
        p_rest = jnp.exp(lp_last - lp)
        a_h = (aa * jnp.exp(lp - lw)).astype(BF16)
        r_h = (r * jnp.exp(lp)).astype(BF16)
        gram = _dot_nt(jnp.concatenate([a_h, r_h], axis=0),
                       jnp.concatenate([_bd(bb * p_inv), _bd(k * p_inv)], axis=0))
        x_list.append(jnp.where(strict, -gram[:CHUNK, :PAIR], 0.0))
        a_ak = jnp.where(strict, gram[:CHUNK, PAIR:], 0.0)
        a_rb = jnp.where(causal, gram[CHUNK:, :PAIR], 0.0).astype(BF16)
        a_rk = jnp.where(causal, gram[CHUNK:, PAIR:], 0.0)
        av = _dot(jnp.concatenate([a_ak, a_rk], axis=0), _bd(v))
        btt = (bb * p_rest).T.astype(BF16)
        z0 = jnp.where(same_head, _dot_tn(k * p_rest, v), 0.0)
        pc = jnp.broadcast_to(jnp.exp(lp_last), (PAIR, PAIR)).T
        rest.append((a_h, r_h, a_rb, av, btt, z0, pc))
    ts = _tri_inverse_many(x_list, ti, tj)
    for n, (t, (a_h, r_h, a_rb, av, btt, z0, pc)) in enumerate(zip(ts, rest)):
        wu = jnp.dot(t, jnp.concatenate([_bd(a_h), _bd(av[:CHUNK])], axis=1),
                     preferred_element_type=F32)
        wr_s[n] = jnp.concatenate([wu[:, :PAIR].astype(BF16), r_h], axis=0)
        ut_s[n] = wu[:, PAIR:]
        arb_s[n] = a_rb
        y0_s[n] = av[CHUNK:]
        btt_s[n] = btt
        z0_s[n] = z0
        pc_s[n] = pc


def _rwkv_body(p_ref, w0_ref, a0_ref, w2_ref, a2_ref, g2_ref, kk_ref, ka_ref, rk_ref,
               lng_ref, lnb_ref, lblk_ref, e128_ref, o_ref,
               r_s, k_s, aa_s, bb_s, lw_s, lp_s, gate_s, y_s, state,
               wr_s, ut_s, arb_s, y0_s, btt_s, z0_s, pc_s):
    tb = MIX_TB

    @pl.when(pl.program_id(1) == 0)
    def _():
        state[...] = jnp.zeros_like(state)

    r = p_ref[0, :, 0:D_MIX]
    k = p_ref[0, :, D_MIX:2 * D_MIX]
    v_cols = 2 * D_MIX
    lora_wa = p_ref[0, :, 3 * D_MIX:3 * D_MIX + LORA_W + LORA_A]
    g_lo = p_ref[0, :, 3 * D_MIX + LORA_W + LORA_A:]
    e128 = e128_ref[...]

    w = -_softplus(-(w0_ref[...] + _dot(jnp.tanh(lora_wa), w2_ref[...]))) - 0.5
    lw = -jnp.exp(w)
    a = _sigmoid(a0_ref[...] + _dot(lora_wa, a2_ref[...]))
    gate_s[...] = _dot(_sigmoid(g_lo), g2_ref[...])
    kk = k * kk_ref[...]
    kk = kk * lax.rsqrt(_head_sums(kk * kk, e128) + L2_EPS)
    k = k * (1.0 + (a - 1.0) * ka_ref[...])
    r_s[...] = r
    k_s[...] = k
    aa_s[...] = -kk
    bb_s[...] = kk * a
    lw_s[...] = lw
    lp_s[...] = _sel_dot(lblk_ref[...], lw)

    masks = _pair_masks()
    n_chunks = tb // CHUNK
    loads = []
    for c in range(n_chunks):
        rows = pl.ds(c * CHUNK, CHUNK)
        for pr in range(N_PAIRS):
            cols = pl.ds(pr * PAIR, PAIR)
            v = p_ref[0, rows, pl.ds(v_cols + pr * PAIR, PAIR)]
            loads.append((r_s[rows, cols], k_s[rows, cols], v, aa_s[rows, cols],
                          bb_s[rows, cols], lw_s[rows, cols], lp_s[rows, cols]))
    _rwkv_precompute(loads, masks, wr_s, ut_s, arb_s, y0_s, btt_s, z0_s, pc_s)

    same_head = masks[-1]
    hs = [state[pr] for pr in range(N_PAIRS)]
    for c in range(n_chunks):
        rows = pl.ds(c * CHUNK, CHUNK)
        wh_rh = [jnp.dot(wr_s[c * N_PAIRS + pr], hs[pr].astype(BF16), preferred_element_type=F32)
                 for pr in range(N_PAIRS)]
        u = [(wh_rh[pr][:CHUNK] + ut_s[c * N_PAIRS + pr]).astype(BF16) for pr in range(N_PAIRS)]
        for pr in range(N_PAIRS):
            n = c * N_PAIRS + pr
            bu = jnp.dot(btt_s[n], u[pr], preferred_element_type=F32)
            hs[pr] = pc_s[n] * hs[pr] + jnp.where(same_head, bu, 0.0) + z0_s[n]
        for pr in range(N_PAIRS):
            n = c * N_PAIRS + pr
            y_s[rows, pl.ds(pr * PAIR, PAIR)] = (
                wh_rh[pr][CHUNK:] + jnp.dot(arb_s[n], _bd(u[pr]), preferred_element_type=F32) + y0_s[n])
    for pr in range(N_PAIRS):
        state[pr] = hs[pr]

    y = y_s[...]
    inv_d = 1.0 / HEAD_DIM
    yc = y - _head_sums(y, e128) * inv_d
    y = yc * lax.rsqrt(_head_sums(yc * yc, e128) * inv_d + GN_EPS) * lng_ref[...] + lnb_ref[...]
    bonus = _head_sums(r_s[...] * k_s[...] * rk_ref[...], e128) * p_ref[0, :, v_cols:v_cols + D_MIX]
    o_ref[0] = (y + bonus) * gate_s[...]


def _rwkv(p, params, consts):
    b, t, _ = p.shape
    tb = MIX_TB
    lblk, e128 = consts
    n_prob = (tb // CHUNK) * N_PAIRS
    const = lambda arr: pl.BlockSpec(arr.shape, lambda i, j: (0,) * arr.ndim)
    smalls = list(params) + [lblk, e128]
    return pl.pallas_call(
        _rwkv_body,
        grid=(b, t // tb),
        in_specs=[pl.BlockSpec((1, tb, RWKV_COLS), lambda i, j: (i, j, 0))] + [const(a) for a in smalls],
        out_specs=pl.BlockSpec((1, tb, D_MIX), lambda i, j: (i, j, 0)),
        out_shape=jax.ShapeDtypeStruct((b, t, D_MIX), F32),
        scratch_shapes=[pltpu.VMEM((tb, D_MIX), F32)] * 8
        + [pltpu.VMEM((N_PAIRS, PAIR, PAIR), F32),
           pltpu.VMEM((n_prob, 2 * CHUNK, PAIR), BF16),
           pltpu.VMEM((n_prob, CHUNK, PAIR), F32),
           pltpu.VMEM((n_prob, CHUNK, PAIR), BF16),
           pltpu.VMEM((n_prob, CHUNK, PAIR), F32),
           pltpu.VMEM((n_prob, PAIR, CHUNK), BF16),
           pltpu.VMEM((n_prob, PAIR, PAIR), F32),
           pltpu.VMEM((n_prob, PAIR, PAIR), F32)],
        compiler_params=pltpu.CompilerParams(
            dimension_semantics=("parallel", "arbitrary"), vmem_limit_bytes=VMEM_LIMIT),
        name="rwkv",
    )(p, *smalls)


def _selection_constants(tb):
    i = jnp.arange(tb)
    lblk = ((i[:, None] // CHUNK == i[None, :] // CHUNK) & (i[:, None] >= i[None, :])).astype(BF16)
    c = jnp.arange(PAIR)
    e128 = (c[:, None] // HEAD_DIM == c[None, :] // HEAD_DIM).astype(BF16)
    return lblk, e128


def kernel(x, ffn1_pre_g, ffn1_w_gate, ffn1_w_up, ffn1_w_down, ffn1_post_g, mix_pre_g, w_in, gdn_conv_w, gdn_a_log, gdn_dt_bias, gdn_norm_g, rwkv_mu, rwkv_w0, rwkv_w2, rwkv_a0, rwkv_a2, rwkv_g2, rwkv_k_k, rwkv_k_a, rwkv_r_k, rwkv_ln_g, rwkv_ln_b, w_out, mix_post_g, ffn2_pre_g, ffn2_w_gate, ffn2_w_up, ffn2_w_down, ffn2_post_g):
    b, t, d = x.shape
    depth = ffn1_pre_g.shape[0]
    consts = _selection_constants(MIX_TB)
    row = lambda v: v.reshape(1, -1).astype(F32)
    h = x.reshape(b * t, d)
    for l in range(depth):
        h = _ffn(h, row(ffn1_pre_g[l]), ffn1_w_gate[l].astype(BF16), ffn1_w_up[l].astype(BF16),
                 ffn1_w_down[l].astype(BF16), row(ffn1_post_g[l]))

        wl = w_in[l]
        n_qkvz = 4 * D_MIX
        w_cat = jnp.concatenate(
            [wl[:, :n_qkvz], wl[:, n_qkvz:n_qkvz + 2 * N_HEADS],
             jnp.zeros((d, GDN_SMALL - 2 * N_HEADS), wl.dtype), wl[:, n_qkvz + 2 * N_HEADS:]], axis=1)
        x_main, x_small, x_rwkv = _in_proj(h, row(mix_pre_g[l]), w_cat.astype(BF16),
                                           gdn_conv_w[l].astype(F32), row(rwkv_mu[l]), t)

        pad_small = lambda v: jnp.zeros((1, GDN_SMALL), F32).at[0, N_HEADS:2 * N_HEADS].set(v)
        y_gdn = _gdn(x_main.reshape(b, t, GDN_MAIN), x_small.reshape(b, t, GDN_SMALL),
                     pad_small(gdn_a_log[l]), pad_small(gdn_dt_bias[l]),
                     row(jnp.tile(gdn_norm_g[l], N_HEADS)), consts)

        zeros_lora = jnp.zeros((LORA_W, D_MIX), F32)
        w2_pad = jnp.concatenate([rwkv_w2[l], zeros_lora], axis=0).astype(BF16)
        a2_pad = jnp.concatenate([zeros_lora, rwkv_a2[l]], axis=0).astype(BF16)
        rwkv_params = (row(rwkv_w0[l]), row(rwkv_a0[l]), w2_pad, a2_pad,
                       rwkv_g2[l].astype(BF16), row(rwkv_k_k[l]), row(rwkv_k_a[l]), row(rwkv_r_k[l]),
                       row(rwkv_ln_g[l]), row(rwkv_ln_b[l]))
        y_rwkv = _rwkv(x_rwkv.reshape(b, t, RWKV_COLS), rwkv_params, consts)

        h = _ffn(h, row(ffn2_pre_g[l]), ffn2_w_gate[l].astype(BF16), ffn2_w_up[l].astype(BF16),
                 ffn2_w_down[l].astype(BF16), row(ffn2_post_g[l]),
                 mix=(y_gdn.reshape(b * t, D_MIX), y_rwkv.reshape(b * t, D_MIX),
                      w_out[l].astype(BF16), row(mix_post_g[l])))
    return h.reshape(b, t, d)
```

```python
import functools

import jax
import jax.numpy as jnp
from jax import lax
from jax.experimental import pallas as pl
from jax.experimental.pallas import tpu as pltpu

F32 = jnp.float32
BF16 = jnp.bfloat16

D_MODEL = 1024
D_FF = 2816
HEAD_DIM = 64
N_HEADS = 8
D_MIX = N_HEADS * HEAD_DIM
N_PAIRS = N_HEADS // 2
PAIR = 2 * HEAD_DIM
CHUNK = 64
CONV_WIDTH = 4
LORA_W = 64
LORA_A = 64
LORA_G = 128
EPS = 1e-6
L2_EPS = 1e-6
GN_EPS = 64e-5

LANES = 128
SUBLANES = 8
GDN_MAIN = 4 * D_MIX
GDN_SMALL = LANES
RWKV_COLS = 3 * D_MIX + LORA_W + LORA_A + LORA_G

FFN_TM = 512
FFN_TF = 1408
PROJ_TM = 512
MIX_TB = 512
RWKV_PREP_ROWS = 256
VMEM_LIMIT = 56 * 1024 * 1024


def _dot(a, b):
    return jnp.dot(a.astype(BF16), b.astype(BF16), preferred_element_type=F32)


def _dot_nt(a, b):
    return lax.dot_general(a.astype(BF16), b.astype(BF16), (((1,), (1,)), ((), ())),
                           preferred_element_type=F32)


def _dot_tn(a, b):
    return lax.dot_general(a.astype(BF16), b.astype(BF16), (((0,), (0,)), ((), ())),
                           preferred_element_type=F32)


def _split(x, passes):
    pieces = []
    rem = x
    for i in range(passes):
        p = rem.astype(BF16)
        pieces.append(p)
        if i + 1 < passes:
            rem = rem - p.astype(F32)
    return pieces


def _head_sums(x, e128):
    xb = x.astype(BF16)
    return jnp.concatenate(
        [jnp.dot(xb[:, g * PAIR:(g + 1) * PAIR], e128, preferred_element_type=F32)
         for g in range(N_PAIRS)], axis=1)


def _sel_dot(sel, x, passes=3):
    out = None
    for p in _split(x, passes):
        t = jnp.dot(sel, p, preferred_element_type=F32)
        out = t if out is None else out + t
    return out


def _rmsnorm(x, g):
    return x * lax.rsqrt(jnp.mean(x * x, axis=-1, keepdims=True) + EPS) * g


def _sigmoid(x):
    return jax.nn.sigmoid(x)


def _silu(x):
    hx = 0.5 * x
    return hx + hx * jnp.tanh(hx)


def _softplus(x):
    return jnp.maximum(x, 0.0) + jnp.log(1.0 + jnp.exp(-jnp.abs(x)))


def _bd(x):
    x = x.astype(BF16)
    lane = lax.broadcasted_iota(jnp.int32, x.shape, 1)
    zero = jnp.zeros_like(x)
    return jnp.concatenate([jnp.where(lane < HEAD_DIM, x, zero),
                            jnp.where(lane >= HEAD_DIM, x, zero)], axis=0)


def _advance(filler):
    if filler is not None:
        next(filler, None)


def _drain(stages):
    for _ in stages:
        pass


def _tri_inverse_many(xs, ti, tj, filler=None):
    def sub_blocks(b):
        return ((ti ^ tj) < 2 * b) & ((ti & b) != 0) & ((tj & b) == 0)

    eye = jnp.where(ti == tj, 1.0, 0.0)
    ts = [(eye - jnp.where(sub_blocks(1), x, 0.0)).astype(BF16) for x in xs]
    xbs = [x.astype(BF16) for x in xs]
    zero = jnp.zeros((CHUNK, PAIR), BF16)
    b = 2
    while b < CHUNK:
        m = sub_blocks(b)
        tls = [jnp.dot(t, _bd(jnp.where(m, xb, zero)), preferred_element_type=F32)
               for t, xb in zip(ts, xbs)]
        _advance(filler)
        ts = [jnp.dot((eye - tl).astype(BF16), _bd(t), preferred_element_type=F32).astype(BF16)
              for tl, t in zip(tls, ts)]
        _advance(filler)
        b *= 2
    return ts


def _pair_masks():
    ti = lax.broadcasted_iota(jnp.int32, (CHUNK, PAIR), 0)
    tj = lax.broadcasted_iota(jnp.int32, (CHUNK, PAIR), 1) & (HEAD_DIM - 1)
    ii = lax.broadcasted_iota(jnp.int32, (PAIR, PAIR), 0)
    jj = lax.broadcasted_iota(jnp.int32, (PAIR, PAIR), 1)
    return ti, tj, ti >= tj, ti > tj, (ii ^ jj) < HEAD_DIM


def _ffn_body(with_mix, *refs):
    if with_mix:
        (yg_ref, yr_ref, h_ref, wout_ref, mixg_ref, preg_ref, wg_ref, wu_ref, wd_ref, postg_ref,
         o_ref, xn_scr, h_scr, acc_scr) = refs
    else:
        (h_ref, preg_ref, wg_ref, wu_ref, wd_ref, postg_ref, o_ref, xn_scr, h_scr, acc_scr) = refs
    j = pl.program_id(1)

    @pl.when(j == 0)
    def _():
        h = h_ref[...]
        if with_mix:
            y = jnp.concatenate([yg_ref[...], yr_ref[...]], axis=-1)
            mix = _dot(y, wout_ref[...])
            h = h + _rmsnorm(mix, mixg_ref[...])
        h_scr[...] = h
        xn_scr[...] = _rmsnorm(h, preg_ref[...]).astype(BF16)
        acc_scr[...] = jnp.zeros_like(acc_scr)

    xn = xn_scr[...]
    gate = jnp.dot(xn, wg_ref[...], preferred_element_type=F32)
    up = jnp.dot(xn, wu_ref[...], preferred_element_type=F32)
    hid = (_silu(gate) * up).astype(BF16)
    acc_scr[...] += jnp.dot(hid, wd_ref[...], preferred_element_type=F32)

    @pl.when(j == pl.num_programs(1) - 1)
    def _():
        o_ref[...] = h_scr[...] + 0.5 * _rmsnorm(acc_scr[...], postg_ref[...])


def _ffn(h, pre_g, w_gate, w_up, w_down, post_g, mix=None):
    m = h.shape[0]
    grid = (m // FFN_TM, D_FF // FFN_TF)
    row = lambda width: pl.BlockSpec((FFN_TM, width), lambda i, j: (i, 0))
    const = lambda shape: pl.BlockSpec(shape, lambda i, j: (0, 0))
    in_specs, args = [], []
    if mix is not None:
        y_gdn, y_rwkv, w_out, mix_g = mix
        in_specs += [row(D_MIX), row(D_MIX)]
        args += [y_gdn, y_rwkv]
    in_specs.append(row(D_MODEL))
    args.append(h)
    if mix is not None:
        in_specs += [const((D_MODEL, D_MODEL)), const((1, D_MODEL))]
        args += [w_out, mix_g]
    in_specs += [const((1, D_MODEL)),
                 pl.BlockSpec((D_MODEL, FFN_TF), lambda i, j: (0, j)),
                 pl.BlockSpec((D_MODEL, FFN_TF), lambda i, j: (0, j)),
                 pl.BlockSpec((FFN_TF, D_MODEL), lambda i, j: (j, 0)),
                 const((1, D_MODEL))]
    args += [pre_g, w_gate, w_up, w_down, post_g]
    return pl.pallas_call(
        functools.partial(_ffn_body, mix is not None),
        grid=grid,
        in_specs=in_specs,
        out_specs=row(D_MODEL),
        out_shape=jax.ShapeDtypeStruct((m, D_MODEL), F32),
        scratch_shapes=[pltpu.VMEM((FFN_TM, D_MODEL), BF16),
                        pltpu.VMEM((FFN_TM, D_MODEL), F32),
                        pltpu.VMEM((FFN_TM, D_MODEL), F32)],
        compiler_params=pltpu.CompilerParams(
            dimension_semantics=("parallel", "arbitrary"), vmem_limit_bytes=VMEM_LIMIT),
        name="ffn_mix" if mix is not None else "ffn",
    )(*args)


def _proj_body(tiles_per_seq, h_ref, g_ref, w_ref, convw_ref, mu_ref, main_ref, small_ref, rwkv_ref,
               qkv_buf, rwkv_buf):
    tm = PROJ_TM

    @pl.when(pl.program_id(0) % tiles_per_seq == 0)
    def _():
        qkv_buf[0:SUBLANES, :] = jnp.zeros((SUBLANES, 3 * D_MIX), F32)
        rwkv_buf[0:SUBLANES, :] = jnp.zeros((SUBLANES, RWKV_COLS), F32)

    xn = _rmsnorm(h_ref[...], g_ref[...]).astype(BF16)
    proj = lambda lo, hi: jnp.dot(xn, w_ref[:, lo:hi], preferred_element_type=F32)

    rwkv_buf[SUBLANES:SUBLANES + tm, :] = proj(GDN_MAIN + GDN_SMALL, GDN_MAIN + GDN_SMALL + RWKV_COLS)
    pall = rwkv_buf[...]
    cur = pall[SUBLANES:]
    prev = pltpu.roll(pall, 1, axis=0)[SUBLANES:]
    rwkv_ref[...] = cur + (prev - cur) * mu_ref[...]

    qkv_buf[SUBLANES:SUBLANES + tm, :] = proj(0, 3 * D_MIX)
    xall = qkv_buf[...]
    conv = xall[SUBLANES:] * convw_ref[CONV_WIDTH - 1:CONV_WIDTH, :]
    for d in range(1, CONV_WIDTH):
        shifted = pltpu.roll(xall, d, axis=0)[SUBLANES:]
        conv = conv + shifted * convw_ref[CONV_WIDTH - 1 - d:CONV_WIDTH - d, :]
    main_ref[:, 0:3 * D_MIX] = _silu(conv)

    main_ref[:, 3 * D_MIX:] = proj(3 * D_MIX, GDN_MAIN)
    small_ref[...] = proj(GDN_MAIN, GDN_MAIN + GDN_SMALL)

    qkv_buf[0:SUBLANES, :] = qkv_buf[tm:tm + SUBLANES, :]
    rwkv_buf[0:SUBLANES, :] = rwkv_buf[tm:tm + SUBLANES, :]


def _in_proj(h, g, w, conv_w, mu, seq_len):
    m = h.shape[0]
    n = w.shape[1]
    row = lambda width: pl.BlockSpec((PROJ_TM, width), lambda i: (i, 0))
    const = lambda arr: pl.BlockSpec(arr.shape, lambda i: (0, 0))
    return pl.pallas_call(
        functools.partial(_proj_body, seq_len // PROJ_TM),
        grid=(m // PROJ_TM,),
        in_specs=[row(D_MODEL), const(g), pl.BlockSpec((D_MODEL, n), lambda i: (0, 0)),
                  const(conv_w), const(mu)],
        out_specs=[row(GDN_MAIN), row(GDN_SMALL), row(RWKV_COLS)],
        out_shape=[jax.ShapeDtypeStruct((m, GDN_MAIN), F32),
                   jax.ShapeDtypeStruct((m, GDN_SMALL), F32),
                   jax.ShapeDtypeStruct((m, RWKV_COLS), F32)],
        scratch_shapes=[pltpu.VMEM((PROJ_TM + SUBLANES, 3 * D_MIX), F32),
                        pltpu.VMEM((PROJ_TM + SUBLANES, RWKV_COLS), F32)],
        compiler_params=pltpu.CompilerParams(
            dimension_semantics=("arbitrary",), vmem_limit_bytes=VMEM_LIMIT),
        name="in_proj",
    )(h, g, w, conv_w, mu)


def _gdn_precompute(loads, masks, wq_s, u_s, attn_s, kdt_s, base=0, filler=None):
    ti, tj, causal, strict, _ = masks
    a_list, rest = [], []
    for q, k, v, beta, g_nat in loads:
        g_last = g_nat[CHUNK - 1:CHUNK, :]
        e_g = jnp.exp(g_nat)
        kb = k * beta
        g_row = jnp.sum(jnp.where(ti == tj, g_nat, 0.0), axis=0, keepdims=True)
        decay = jnp.exp(jnp.where(causal, g_nat - g_row, -jnp.inf))
        gram = _dot_nt(jnp.concatenate([kb, q], axis=0), _bd(k))
        a_list.append(jnp.where(strict, gram[:CHUNK] * decay, 0.0))
        attn = (gram[CHUNK:] * decay).astype(BF16)
        rhs = jnp.concatenate([_bd(v * beta), _bd(kb * e_g)], axis=1)
        kdt = (k * jnp.exp(g_last - g_nat)).T.astype(BF16)
        rest.append((attn, rhs, (q * e_g).astype(BF16), kdt))
    ts = _tri_inverse_many(a_list, ti, tj, filler)
    for n, (t, (attn, rhs, qg, kdt)) in enumerate(zip(ts, rest), start=base):
        uw = jnp.dot(t, rhs, preferred_element_type=F32)
        wq_s[n] = jnp.concatenate([uw[:, PAIR:].astype(BF16), qg], axis=0)
        u_s[n] = uw[:, :PAIR]
        attn_s[n] = attn
        kdt_s[n] = kdt


def _gdn_body(xm_ref, xs_ref, alog_ref, dtb_ref, ng_ref, lblk_ref, e128_ref,
              o_ref, q_s, k_s, beta_s, g_s, o_s, state,
              wq_s, u_s, attn_s, kdt_s):
    tb = MIX_TB

    @pl.when(pl.program_id(1) == 0)
    def _():
        state[...] = jnp.zeros_like(state)

    e128 = e128_ref[...]
    masks = _pair_masks()
    same_head = masks[-1]
    n_chunks = tb // CHUNK
    first_head = lax.broadcasted_iota(jnp.int32, (CHUNK, LANES), 1) < HEAD_DIM
    lane_bcast = lambda x, j: jnp.broadcast_to(x[:, j:j + 1], (CHUNK, LANES))
    g_last = {}

    def prepare(chunks):
        for c in chunks:
            rows = pl.ds(c * CHUNK, CHUNK)
            q = xm_ref[0, rows, 0:D_MIX]
            k = xm_ref[0, rows, D_MIX:2 * D_MIX]
            q_s[rows, :] = q * lax.rsqrt(_head_sums(q * q, e128) + L2_EPS) * (HEAD_DIM ** -0.5)
            k_s[rows, :] = k * lax.rsqrt(_head_sums(k * k, e128) + L2_EPS)
            xs = xs_ref[0, rows, :]
            beta = _sigmoid(xs)
            g_small = -jnp.exp(alog_ref[...]) * _softplus(xs + dtb_ref[...])
            g_cum = _sel_dot(lblk_ref[...], g_small)
            g_last[c] = g_cum[CHUNK - 1:CHUNK, :]
            for p in range(N_PAIRS):
                cols = pl.ds(p * PAIR, PAIR)
                g_s[rows, cols] = jnp.where(first_head, lane_bcast(g_cum, N_HEADS + 2 * p),
                                            lane_bcast(g_cum, N_HEADS + 2 * p + 1))
                beta_s[rows, cols] = jnp.where(first_head, lane_bcast(beta, 2 * p),
                                               lane_bcast(beta, 2 * p + 1))
            yield

    def finish(chunks):
        for c in chunks:
            rows = pl.ds(c * CHUNK, CHUNK)
            o = o_s[rows, :]
            z = xm_ref[0, rows, 3 * D_MIX:4 * D_MIX]
            ms = _head_sums(o * o, e128) * (1.0 / HEAD_DIM)
            o_ref[0, rows, :] = o * lax.rsqrt(ms + EPS) * ng_ref[...] * _silu(z)
            yield

    def loads_of(chunks):
        loads = []
        for c in chunks:
            rows = pl.ds(c * CHUNK, CHUNK)
            for p in range(N_PAIRS):
                cols = pl.ds(p * PAIR, PAIR)
                v = xm_ref[0, rows, pl.ds(2 * D_MIX + p * PAIR, PAIR)]
                loads.append((q_s[rows, cols], k_s[rows, cols], v, beta_s[rows, cols], g_s[rows, cols]))
        return loads

    s = [state[p] for p in range(N_PAIRS)]

    def sequential(chunks):
        for c in chunks:
            rows = pl.ds(c * CHUNK, CHUNK)
            ws_qs = [jnp.dot(wq_s[c * N_PAIRS + p], s[p].astype(BF16), preferred_element_type=F32)
                     for p in range(N_PAIRS)]
            yield
            v_new = [(u_s[c * N_PAIRS + p] - ws_qs[p][:CHUNK]).astype(BF16) for p in range(N_PAIRS)]
            e_last = jnp.exp(g_last[c])
            for p in range(N_PAIRS):
                head_decay = lambda h: jnp.broadcast_to(
                    e_last[:, N_HEADS + h:N_HEADS + h + 1], (HEAD_DIM, PAIR))
                s_decay = jnp.concatenate([head_decay(2 * p), head_decay(2 * p + 1)], axis=0)
                kv = jnp.dot(kdt_s[c * N_PAIRS + p], v_new[p], preferred_element_type=F32)
                s[p] = s[p] * s_decay + jnp.where(same_head, kv, 0.0)
            yield
            for p in range(N_PAIRS):
                o_s[rows, pl.ds(p * PAIR, PAIR)] = ws_qs[p][CHUNK:] + jnp.dot(
                    attn_s[c * N_PAIRS + p], _bd(v_new[p]), preferred_element_type=F32)
            yield

    half = n_chunks // 2
    first, second = range(half), range(half, n_chunks)
    scratch = (wq_s, u_s, attn_s, kdt_s)
    _drain(prepare(first))
    filler = prepare(second)
    _gdn_precompute(loads_of(first), masks, *scratch, filler=filler)
    _drain(filler)
    filler = sequential(first)
    _gdn_precompute(loads_of(second), masks, *scratch, base=half * N_PAIRS, filler=filler)
    _drain(filler)
    filler = finish(first)
    for _ in sequential(second):
        _advance(filler)
    _drain(filler)
    _drain(finish(second))
    for p in range(N_PAIRS):
        state[p] = s[p]


def _gdn(x_main, x_small, a_log_pad, dt_bias_pad, norm_g_full, consts):
    b, t, _ = x_main.shape
    tb = MIX_TB
    lblk, e128 = consts
    n_prob = (tb // CHUNK) * N_PAIRS
    const = lambda arr: pl.BlockSpec(arr.shape, lambda i, j: (0,) * arr.ndim)
    smalls = [a_log_pad, dt_bias_pad, norm_g_full, lblk, e128]
    return pl.pallas_call(
        _gdn_body,
        grid=(b, t // tb),
        in_specs=[pl.BlockSpec((1, tb, GDN_MAIN), lambda i, j: (i, j, 0)),
                  pl.BlockSpec((1, tb, GDN_SMALL), lambda i, j: (i, j, 0))] + [const(a) for a in smalls],
        out_specs=pl.BlockSpec((1, tb, D_MIX), lambda i, j: (i, j, 0)),
        out_shape=jax.ShapeDtypeStruct((b, t, D_MIX), F32),
        scratch_shapes=[pltpu.VMEM((tb, D_MIX), F32)] * 5
        + [pltpu.VMEM((N_PAIRS, PAIR, PAIR), F32),
           pltpu.VMEM((n_prob, 2 * CHUNK, PAIR), BF16),
           pltpu.VMEM((n_prob, CHUNK, PAIR), F32),
           pltpu.VMEM((n_prob, CHUNK, PAIR), BF16),
           pltpu.VMEM((n_prob, PAIR, CHUNK), BF16)],
        compiler_params=pltpu.CompilerParams(
            dimension_semantics=("parallel", "arbitrary"), vmem_limit_bytes=VMEM_LIMIT),
        name="gdn",
    )(x_main, x_small, *smalls)


def _rwkv_precompute(loads, masks, wr_s, ut_s, arb_s, y0_s, btt_s, z0_s, pc_s, base=0, filler=None):
    ti, tj, causal, strict, same_head = masks
    x_list, rest = [], []
    for r, k, v, aa, bb, lw, lp in loads:
        p_inv = jnp.exp(-lp)
        lp_last = lp[CHUNK - 1:CHUNK, :]
        p_rest = jnp.exp(lp_last - lp)
        a_h = (aa * jnp.exp(lp - lw)).astype(BF16)
        r_h = (r * jnp.exp(lp)).astype(BF16)
        gram = _dot_nt(jnp.concatenate([a_h, r_h], axis=0),
                       jnp.concatenate([_bd(bb * p_inv), _bd(k * p_inv)], axis=0))
        x_list.append(jnp.where(strict, -gram[:CHUNK, :PAIR], 0.0))
        a_ak = jnp.where(strict, gram[:CHUNK, PAIR:], 0.0)
        a_rb = jnp.where(causal, gram[CHUNK:, :PAIR], 0.0).astype(BF16)
        a_rk = jnp.where(causal, gram[CHUNK:, PAIR:], 0.0)
        av = _dot(jnp.concatenate([a_ak, a_rk], axis=0), _bd(v))
        btt = (bb * p_rest).T.astype(BF16)
        z0 = jnp.where(same_head, _dot_tn(k * p_rest, v), 0.0)
        pc = jnp.broadcast_to(jnp.exp(lp_last), (PAIR, PAIR)).T
        rest.append((a_h, r_h, a_rb, av, btt, z0, pc))
    ts = _tri_inverse_many(x_list, ti, tj, filler)
    for n, (t, (a_h, r_h, a_rb, av, btt, z0, pc)) in enumerate(zip(ts, rest), start=base):
        wu = jnp.dot(t, jnp.concatenate([_bd(a_h), _bd(av[:CHUNK])], axis=1),
                     preferred_element_type=F32)
        wr_s[n] = jnp.concatenate([wu[:, :PAIR].astype(BF16), r_h], axis=0)
        ut_s[n] = wu[:, PAIR:]
        arb_s[n] = a_rb
        y0_s[n] = av[CHUNK:]
        btt_s[n] = btt
        z0_s[n] = z0
        pc_s[n] = pc


def _rwkv_body(p_ref, w0_ref, a0_ref, w2_ref, a2_ref, g2_ref, kk_ref, ka_ref, rk_ref,
               lng_ref, lnb_ref, lblk_ref, e128_ref, o_ref,
               k_s, aa_s, bb_s, lw_s, lp_s, gate_s, y_s, state,
               wr_s, ut_s, arb_s, y0_s, btt_s, z0_s, pc_s):
    tb = MIX_TB

    @pl.when(pl.program_id(1) == 0)
    def _():
        state[...] = jnp.zeros_like(state)

    v_cols = 2 * D_MIX
    e128 = e128_ref[...]
    masks = _pair_masks()
    same_head = masks[-1]
    n_chunks = tb // CHUNK

    def prepare(chunks):
        slab = lblk_ref.shape[0]
        for r0 in range(chunks[0] * CHUNK, (chunks[-1] + 1) * CHUNK, slab):
            rows = pl.ds(r0, slab)
            k = p_ref[0, rows, D_MIX:2 * D_MIX]
            lora_wa = p_ref[0, rows, 3 * D_MIX:3 * D_MIX + LORA_W + LORA_A]
            g_lo = p_ref[0, rows, 3 * D_MIX + LORA_W + LORA_A:]
            w = -_softplus(-(w0_ref[...] + _dot(jnp.tanh(lora_wa), w2_ref[...]))) - 0.5
            lw = -jnp.exp(w)
            a = _sigmoid(a0_ref[...] + _dot(lora_wa, a2_ref[...]))
            gate_s[rows, :] = _dot(_sigmoid(g_lo), g2_ref[...])
            kk = k * kk_ref[...]
            kk = kk * lax.rsqrt(_head_sums(kk * kk, e128) + L2_EPS)
            k_s[rows, :] = k * (1.0 + (a - 1.0) * ka_ref[...])
            aa_s[rows, :] = -kk
            bb_s[rows, :] = kk * a
            lw_s[rows, :] = lw
            lp_s[rows, :] = _sel_dot(lblk_ref[...], lw)
            yield

    def finish(chunks):
        inv_d = 1.0 / HEAD_DIM
        for c in chunks:
            rows = pl.ds(c * CHUNK, CHUNK)
            y = y_s[rows, :]
            yc = y - _head_sums(y, e128) * inv_d
            y = yc * lax.rsqrt(_head_sums(yc * yc, e128) * inv_d + GN_EPS) * lng_ref[...] + lnb_ref[...]
            r = p_ref[0, rows, 0:D_MIX]
            bonus = _head_sums(r * k_s[rows, :] * rk_ref[...], e128) * p_ref[0, rows, v_cols:v_cols + D_MIX]
            o_ref[0, rows, :] = (y + bonus) * gate_s[rows, :]
            yield

    def loads_of(chunks):
        loads = []
        for c in chunks:
            rows = pl.ds(c * CHUNK, CHUNK)
            for pr in range(N_PAIRS):
                cols = pl.ds(pr * PAIR, PAIR)
                r = p_ref[0, rows, cols]
                v = p_ref[0, rows, pl.ds(v_cols + pr * PAIR, PAIR)]
                loads.append((r, k_s[rows, cols], v, aa_s[rows, cols],
                              bb_s[rows, cols], lw_s[rows, cols], lp_s[rows, cols]))
        return loads

    hs = [state[pr] for pr in range(N_PAIRS)]

    def sequential(chunks):
        for c in chunks:
            rows = pl.ds(c * CHUNK, CHUNK)
            wh_rh = [jnp.dot(wr_s[c * N_PAIRS + pr], hs[pr].astype(BF16), preferred_element_type=F32)
                     for pr in range(N_PAIRS)]
            yield
            u = [(wh_rh[pr][:CHUNK] + ut_s[c * N_PAIRS + pr]).astype(BF16) for pr in range(N_PAIRS)]
            for pr in range(N_PAIRS):
                n = c * N_PAIRS + pr
                bu = jnp.dot(btt_s[n], u[pr], preferred_element_type=F32)
                hs[pr] = pc_s[n] * hs[pr] + jnp.where(same_head, bu, 0.0) + z0_s[n]
            yield
            for pr in range(N_PAIRS):
                n = c * N_PAIRS + pr
                y_s[rows, pl.ds(pr * PAIR, PAIR)] = (
                    wh_rh[pr][CHUNK:] + jnp.dot(arb_s[n], _bd(u[pr]), preferred_element_type=F32)
                    + y0_s[n])
            yield

    half = n_chunks // 2
    first, second = range(half), range(half, n_chunks)
    scratch = (wr_s, ut_s, arb_s, y0_s, btt_s, z0_s, pc_s)
    _drain(prepare(first))
    filler = prepare(second)
    _rwkv_precompute(loads_of(first), masks, *scratch, filler=filler)
    _drain(filler)
    filler = sequential(first)
    _rwkv_precompute(loads_of(second), masks, *scratch, base=half * N_PAIRS, filler=filler)
    _drain(filler)
    filler = finish(first)
    for _ in sequential(second):
        _advance(filler)
    _drain(filler)
    _drain(finish(second))
    for pr in range(N_PAIRS):
        state[pr] = hs[pr]


def _rwkv(p, params, consts):
    b, t, _ = p.shape
    tb = MIX_TB
    lblk, e128 = consts
    n_prob = (tb // CHUNK) * N_PAIRS
    const = lambda arr: pl.BlockSpec(arr.shape, lambda i, j: (0,) * arr.ndim)
    smalls = list(params) + [lblk, e128]
    return pl.pallas_call(
        _rwkv_body,
        grid=(b, t // tb),
        in_specs=[pl.BlockSpec((1, tb, RWKV_COLS), lambda i, j: (i, j, 0))] + [const(a) for a in smalls],
        out_specs=pl.BlockSpec((1, tb, D_MIX), lambda i, j: (i, j, 0)),
        out_shape=jax.ShapeDtypeStruct((b, t, D_MIX), F32),
        scratch_shapes=[pltpu.VMEM((tb, D_MIX), F32)] * 7
        + [pltpu.VMEM((N_PAIRS, PAIR, PAIR), F32),
           pltpu.VMEM((n_prob, 2 * CHUNK, PAIR), BF16),
           pltpu.VMEM((n_prob, CHUNK, PAIR), F32),
           pltpu.VMEM((n_prob, CHUNK, PAIR), BF16),
           pltpu.VMEM((n_prob, CHUNK, PAIR), F32),
           pltpu.VMEM((n_prob, PAIR, CHUNK), BF16),
           pltpu.VMEM((n_prob, PAIR, PAIR), F32),
           pltpu.VMEM((n_prob, PAIR, PAIR), F32)],
        compiler_params=pltpu.CompilerParams(
            dimension_semantics=("parallel", "arbitrary"), vmem_limit_bytes=VMEM_LIMIT),
        name="rwkv",
    )(p, *smalls)


def _selection_constants(tb):
    i = jnp.arange(tb)
    lblk = ((i[:, None] // CHUNK == i[None, :] // CHUNK) & (i[:, None] >= i[None, :])).astype(BF16)
    c = jnp.arange(PAIR)
    e128 = (c[:, None] // HEAD_DIM == c[None, :] // HEAD_DIM).astype(BF16)
    return lblk, e128


def kernel(x, ffn1_pre_g, ffn1_w_gate, ffn1_w_up, ffn1_w_down, ffn1_post_g, mix_pre_g, w_in, gdn_conv_w, gdn_a_log, gdn_dt_bias, gdn_norm_g, rwkv_mu, rwkv_w0, rwkv_w2, rwkv_a0, rwkv_a2, rwkv_g2, rwkv_k_k, rwkv_k_a, rwkv_r_k, rwkv_ln_g, rwkv_ln_b, w_out, mix_post_g, ffn2_pre_g, ffn2_w_gate, ffn2_w_up, ffn2_w_down, ffn2_post_g):
    b, t, d = x.shape
    depth = ffn1_pre_g.shape[0]
    consts = _selection_constants(CHUNK)
    rwkv_consts = _selection_constants(RWKV_PREP_ROWS)
    row = lambda v: v.reshape(1, -1).astype(F32)
    h = x.reshape(b * t, d)
    for l in range(depth):
        h = _ffn(h, row(ffn1_pre_g[l]), ffn1_w_gate[l].astype(BF16), ffn1_w_up[l].astype(BF16),
                 ffn1_w_down[l].astype(BF16), row(ffn1_post_g[l]))

        wl = w_in[l]
        n_qkvz = 4 * D_MIX
        w_cat = jnp.concatenate(
            [wl[:, :n_qkvz], wl[:, n_qkvz:n_qkvz + 2 * N_HEADS],
             jnp.zeros((d, GDN_SMALL - 2 * N_HEADS), wl.dtype), wl[:, n_qkvz + 2 * N_HEADS:]], axis=1)
        x_main, x_small, x_rwkv = _in_proj(h, row(mix_pre_g[l]), w_cat.astype(BF16),
                                           gdn_conv_w[l].astype(F32), row(rwkv_mu[l]), t)

        pad_small = lambda v: jnp.zeros((1, GDN_SMALL), F32).at[0, N_HEADS:2 * N_HEADS].set(v)
        y_gdn = _gdn(x_main.reshape(b, t, GDN_MAIN), x_small.reshape(b, t, GDN_SMALL),
                     pad_small(gdn_a_log[l]), pad_small(gdn_dt_bias[l]),
                     row(jnp.tile(gdn_norm_g[l], N_HEADS)), consts)

        zeros_lora = jnp.zeros((LORA_W, D_MIX), F32)
        w2_pad = jnp.concatenate([rwkv_w2[l], zeros_lora], axis=0).astype(BF16)
        a2_pad = jnp.concatenate([zeros_lora, rwkv_a2[l]], axis=0).astype(BF16)
        rwkv_params = (row(rwkv_w0[l]), row(rwkv_a0[l]), w2_pad, a2_pad,
                       rwkv_g2[l].astype(BF16), row(rwkv_k_k[l]), row(rwkv_k_a[l]), row(rwkv_r_k[l]),
                       row(rwkv_ln_g[l]), row(rwkv_ln_b[l]))
        y_rwkv = _rwkv(x_rwkv.reshape(b, t, RWKV_COLS), rwkv_params, rwkv_consts)

        h = _ffn(h, row(ffn2_pre_g[l]), ffn2_w_gate[l].astype(BF16), ffn2_w_up[l].astype(BF16),
                 ffn2_w_down[l].astype(BF16), row(ffn2_post_g[l]),
                 mix=(y_gdn.reshape(b * t, D_MIX), y_rwkv.reshape(b * t, D_MIX),
                      w_out[l].astype(BF16), row(mix_post_g[l])))
    return h.reshape(b, t, d)
```

```python
import functools

import jax
import jax.numpy as jnp
from jax import lax
from jax.experimental import pallas as pl
from jax.experimental.pallas import tpu as pltpu

F32 = jnp.float32
BF16 = jnp.bfloat16

D_MODEL = 1024
D_FF = 2816
HEAD_DIM = 64
N_HEADS = 8
D_MIX = N_HEADS * HEAD_DIM
N_PAIRS = N_HEADS // 2
PAIR = 2 * HEAD_DIM
CHUNK = 64
CONV_WIDTH = 4
LORA_W = 64
LORA_A = 64
LORA_G = 128
EPS = 1e-6
L2_EPS = 1e-6
GN_EPS = 64e-5

LANES = 128
SUBLANES = 8
GDN_MAIN = 4 * D_MIX
GDN_SMALL = LANES
RWKV_COLS = 3 * D_MIX + LORA_W + LORA_A + LORA_G

FFN_TM = 512
FFN_TF = 1408
PROJ_TM = 512
MIX_TB = 512
RWKV_PREP_ROWS = 256
VMEM_LIMIT = 56 * 1024 * 1024


def _dot(a, b):
    return jnp.dot(a.astype(BF16), b.astype(BF16), preferred_element_type=F32)


def _dot_nt(a, b):
    return lax.dot_general(a.astype(BF16), b.astype(BF16), (((1,), (1,)), ((), ())),
                           preferred_element_type=F32)


def _dot_tn(a, b):
    return lax.dot_general(a.astype(BF16), b.astype(BF16), (((0,), (0,)), ((), ())),
                           preferred_element_type=F32)


def _split(x, passes):
    pieces = []
    rem = x
    for i in range(passes):
        p = rem.astype(BF16)
        pieces.append(p)
        if i + 1 < passes:
            rem = rem - p.astype(F32)
    return pieces


def _head_sums(x, e128):
    xb = x.astype(BF16)
    return jnp.concatenate(
        [jnp.dot(xb[:, g * PAIR:(g + 1) * PAIR], e128, preferred_element_type=F32)
         for g in range(N_PAIRS)], axis=1)


def _sel_dot(sel, x, passes=3):
    out = None
    for p in _split(x, passes):
        t = jnp.dot(sel, p, preferred_element_type=F32)
        out = t if out is None else out + t
    return out


def _rmsnorm(x, g):
    return x * lax.rsqrt(jnp.mean(x * x, axis=-1, keepdims=True) + EPS) * g


def _sigmoid(x):
    return jax.nn.sigmoid(x)


def _silu(x):
    hx = 0.5 * x
    return hx + hx * jnp.tanh(hx)


def _softplus(x):
    return jnp.maximum(x, 0.0) + jnp.log(1.0 + jnp.exp(-jnp.abs(x)))


def _bd(x):
    x = x.astype(BF16)
    lane = lax.broadcasted_iota(jnp.int32, x.shape, 1)
    zero = jnp.zeros_like(x)
    return jnp.concatenate([jnp.where(lane < HEAD_DIM, x, zero),
                            jnp.where(lane >= HEAD_DIM, x, zero)], axis=0)


def _advance(filler):
    if filler is not None:
        next(filler, None)


def _drain(stages):
    for _ in stages:
        pass


def _tri_inverse_many(xs, ti, tj, filler=None):
    def sub_blocks(b):
        return ((ti ^ tj) < 2 * b) & ((ti & b) != 0) & ((tj & b) == 0)

    eye = jnp.where(ti == tj, 1.0, 0.0)
    ts = [(eye - jnp.where(sub_blocks(1), x, 0.0)).astype(BF16) for x in xs]
    xbs = [x.astype(BF16) for x in xs]
    zero = jnp.zeros((CHUNK, PAIR), BF16)
    b = 2
    while b < CHUNK:
        m = sub_blocks(b)
        tls = [jnp.dot(t, _bd(jnp.where(m, xb, zero)), preferred_element_type=F32)
               for t, xb in zip(ts, xbs)]
        _advance(filler)
        ts = [jnp.dot((eye - tl).astype(BF16), _bd(t), preferred_element_type=F32).astype(BF16)
              for tl, t in zip(tls, ts)]
        _advance(filler)
        b *= 2
    return ts


def _pair_masks():
    ti = lax.broadcasted_iota(jnp.int32, (CHUNK, PAIR), 0)
    tj = lax.broadcasted_iota(jnp.int32, (CHUNK, PAIR), 1) & (HEAD_DIM - 1)
    ii = lax.broadcasted_iota(jnp.int32, (PAIR, PAIR), 0)
    jj = lax.broadcasted_iota(jnp.int32, (PAIR, PAIR), 1)
    return ti, tj, ti >= tj, ti > tj, (ii ^ jj) < HEAD_DIM


def _ffn_body(with_mix, *refs):
    if with_mix:
        (yg_ref, yr_ref, h_ref, wout_ref, mixg_ref, preg_ref, wg_ref, wu_ref, wd_ref, postg_ref,
         o_ref) = refs
    else:
        (h_ref, preg_ref, wg_ref, wu_ref, wd_ref, postg_ref, o_ref) = refs

    def half(rows):
        h = h_ref[rows, :]
        if with_mix:
            y = jnp.concatenate([yg_ref[rows, :], yr_ref[rows, :]], axis=-1)
            h = h + _rmsnorm(_dot(y, wout_ref[...]), mixg_ref[...])
        xn = _rmsnorm(h, preg_ref[...]).astype(BF16)
        yield
        gate = jnp.dot(xn, wg_ref[...], preferred_element_type=F32)
        up = jnp.dot(xn, wu_ref[...], preferred_element_type=F32)
        yield
        hid = (_silu(gate) * up).astype(BF16)
        f = jnp.dot(hid, wd_ref[...], preferred_element_type=F32)
        yield
        o_ref[rows, :] = h + 0.5 * _rmsnorm(f, postg_ref[...])
        yield

    rows_per_half = FFN_TM // 2
    first = half(pl.ds(0, rows_per_half))
    second = half(pl.ds(rows_per_half, rows_per_half))
    _advance(first)
    for _ in first:
        _advance(second)
    _drain(second)


def _ffn(h, pre_g, w_gate, w_up, w_down, post_g, mix=None):
    m = h.shape[0]
    row = lambda width: pl.BlockSpec((FFN_TM, width), lambda i: (i, 0))
    resident = pl.BlockSpec(memory_space=pltpu.VMEM)
    in_specs, args = [], []
    if mix is not None:
        y_gdn, y_rwkv, w_out, mix_g = mix
        in_specs += [row(D_MIX), row(D_MIX)]
        args += [y_gdn, y_rwkv]
    in_specs.append(row(D_MODEL))
    args.append(h)
    if mix is not None:
        in_specs += [resident, resident]
        args += [w_out, mix_g]
    in_specs += [resident] * 5
    args += [pre_g, w_gate, w_up, w_down, post_g]
    return pl.pallas_call(
        functools.partial(_ffn_body, mix is not None),
        grid=(m // FFN_TM,),
        in_specs=in_specs,
        out_specs=row(D_MODEL),
        out_shape=jax.ShapeDtypeStruct((m, D_MODEL), F32),
        compiler_params=pltpu.CompilerParams(
            dimension_semantics=("parallel",), vmem_limit_bytes=VMEM_LIMIT),
        name="ffn_mix" if mix is not None else "ffn",
    )(*args)


def _proj_body(tiles_per_seq, h_ref, g_ref, w_ref, convw_ref, mu_ref, main_ref, small_ref, rwkv_ref,
               qkv_buf, rwkv_buf):
    tm = PROJ_TM

    @pl.when(pl.program_id(0) % tiles_per_seq == 0)
    def _():
        qkv_buf[0:SUBLANES, :] = jnp.zeros((SUBLANES, 3 * D_MIX), F32)
        rwkv_buf[0:SUBLANES, :] = jnp.zeros((SUBLANES, RWKV_COLS), F32)

    def half(r0, n):
        rows = pl.ds(r0, n)
        stored = pl.ds(SUBLANES + r0, n)
        window = pl.ds(r0, SUBLANES + n)
        xn = _rmsnorm(h_ref[rows, :], g_ref[...]).astype(BF16)
        proj = lambda lo, hi: jnp.dot(xn, w_ref[:, lo:hi], preferred_element_type=F32)
        yield
        rwkv_buf[stored, :] = proj(GDN_MAIN + GDN_SMALL, GDN_MAIN + GDN_SMALL + RWKV_COLS)
        qkv_buf[stored, :] = proj(0, 3 * D_MIX)
        yield
        pall = rwkv_buf[window, :]
        cur = pall[SUBLANES:]
        prev = pltpu.roll(pall, 1, axis=0)[SUBLANES:]
        rwkv_ref[rows, :] = cur + (prev - cur) * mu_ref[...]
        xall = qkv_buf[window, :]
        conv = xall[SUBLANES:] * convw_ref[CONV_WIDTH - 1:CONV_WIDTH, :]
        for d in range(1, CONV_WIDTH):
            shifted = pltpu.roll(xall, d, axis=0)[SUBLANES:]
            conv = conv + shifted * convw_ref[CONV_WIDTH - 1 - d:CONV_WIDTH - d, :]
        main_ref[rows, 0:3 * D_MIX] = _silu(conv)
        main_ref[rows, 3 * D_MIX:] = proj(3 * D_MIX, GDN_MAIN)
        small_ref[rows, :] = proj(GDN_MAIN, GDN_MAIN + GDN_SMALL)
        yield

    first = half(0, tm // 2)
    second = half(tm // 2, tm // 2)
    _advance(first)
    for _ in first:
        _advance(second)
    _drain(second)

    qkv_buf[0:SUBLANES, :] = qkv_buf[tm:tm + SUBLANES, :]
    rwkv_buf[0:SUBLANES, :] = rwkv_buf[tm:tm + SUBLANES, :]


def _in_proj(h, g, w, conv_w, mu, seq_len):
    m = h.shape[0]
    n = w.shape[1]
    row = lambda width: pl.BlockSpec((PROJ_TM, width), lambda i: (i, 0))
    const = lambda arr: pl.BlockSpec(arr.shape, lambda i: (0, 0))
    return pl.pallas_call(
        functools.partial(_proj_body, seq_len // PROJ_TM),
        grid=(m // PROJ_TM,),
        in_specs=[row(D_MODEL), const(g), pl.BlockSpec((D_MODEL, n), lambda i: (0, 0)),
                  const(conv_w), const(mu)],
        out_specs=[row(GDN_MAIN), row(GDN_SMALL), row(RWKV_COLS)],
        out_shape=[jax.ShapeDtypeStruct((m, GDN_MAIN), F32),
                   jax.ShapeDtypeStruct((m, GDN_SMALL), F32),
                   jax.ShapeDtypeStruct((m, RWKV_COLS), F32)],
        scratch_shapes=[pltpu.VMEM((PROJ_TM + SUBLANES, 3 * D_MIX), F32),
                        pltpu.VMEM((PROJ_TM + SUBLANES, RWKV_COLS), F32)],
        compiler_params=pltpu.CompilerParams(
            dimension_semantics=("arbitrary",), vmem_limit_bytes=VMEM_LIMIT),
        name="in_proj",
    )(h, g, w, conv_w, mu)


def _gdn_precompute(loads, masks, wq_s, u_s, attn_s, kdt_s, base=0, filler=None):
    ti, tj, causal, strict, _ = masks
    a_list, rest = [], []
    for q, k, v, beta, g_nat in loads:
        g_last = g_nat[CHUNK - 1:CHUNK, :]
        e_g = jnp.exp(g_nat)
        kb = k * beta
        g_row = jnp.sum(jnp.where(ti == tj, g_nat, 0.0), axis=0, keepdims=True)
        decay = jnp.exp(jnp.where(causal, g_nat - g_row, -jnp.inf))
        gram = _dot_nt(jnp.concatenate([kb, q], axis=0), _bd(k))
        a_list.append(jnp.where(strict, gram[:CHUNK] * decay, 0.0))
        attn = (gram[CHUNK:] * decay).astype(BF16)
        rhs = jnp.concatenate([_bd(v * beta), _bd(kb * e_g)], axis=1)
        kdt = (k * jnp.exp(g_last - g_nat)).T.astype(BF16)
        rest.append((attn, rhs, (q * e_g).astype(BF16), kdt))
    ts = _tri_inverse_many(a_list, ti, tj, filler)
    for n, (t, (attn, rhs, qg, kdt)) in enumerate(zip(ts, rest), start=base):
        uw = jnp.dot(t, rhs, preferred_element_type=F32)
        wq_s[n] = jnp.concatenate([uw[:, PAIR:].astype(BF16), qg], axis=0)
        u_s[n] = uw[:, :PAIR]
        attn_s[n] = attn
        kdt_s[n] = kdt


def _gdn_body(xm_ref, xs_ref, alog_ref, dtb_ref, ng_ref, lblk_ref, e128_ref,
              o_ref, q_s, k_s, beta_s, g_s, o_s, state,
              wq_s, u_s, attn_s, kdt_s):
    tb = MIX_TB

    @pl.when(pl.program_id(1) == 0)
    def _():
        state[...] = jnp.zeros_like(state)

    e128 = e128_ref[...]
    masks = _pair_masks()
    same_head = masks[-1]
    n_chunks = tb // CHUNK
    first_head = lax.broadcasted_iota(jnp.int32, (CHUNK, LANES), 1) < HEAD_DIM
    lane_bcast = lambda x, j: jnp.broadcast_to(x[:, j:j + 1], (CHUNK, LANES))
    g_last = {}

    def prepare(chunks):
        for c in chunks:
            rows = pl.ds(c * CHUNK, CHUNK)
            q = xm_ref[0, rows, 0:D_MIX]
            k = xm_ref[0, rows, D_MIX:2 * D_MIX]
            q_s[rows, :] = q * lax.rsqrt(_head_sums(q * q, e128) + L2_EPS) * (HEAD_DIM ** -0.5)
            k_s[rows, :] = k * lax.rsqrt(_head_sums(k * k, e128) + L2_EPS)
            xs = xs_ref[0, rows, :]
            beta = _sigmoid(xs)
            g_small = -jnp.exp(alog_ref[...]) * _softplus(xs + dtb_ref[...])
            g_cum = _sel_dot(lblk_ref[...], g_small)
            g_last[c] = g_cum[CHUNK - 1:CHUNK, :]
            for p in range(N_PAIRS):
                cols = pl.ds(p * PAIR, PAIR)
                g_s[rows, cols] = jnp.where(first_head, lane_bcast(g_cum, N_HEADS + 2 * p),
                                            lane_bcast(g_cum, N_HEADS + 2 * p + 1))
                beta_s[rows, cols] = jnp.where(first_head, lane_bcast(beta, 2 * p),
                                               lane_bcast(beta, 2 * p + 1))
            yield

    def finish(chunks):
        for c in chunks:
            rows = pl.ds(c * CHUNK, CHUNK)
            o = o_s[rows, :]
            z = xm_ref[0, rows, 3 * D_MIX:4 * D_MIX]
            ms = _head_sums(o * o, e128) * (1.0 / HEAD_DIM)
            o_ref[0, rows, :] = o * lax.rsqrt(ms + EPS) * ng_ref[...] * _silu(z)
            yield

    def loads_of(chunks):
        loads = []
        for c in chunks:
            rows = pl.ds(c * CHUNK, CHUNK)
            for p in range(N_PAIRS):
                cols = pl.ds(p * PAIR, PAIR)
                v = xm_ref[0, rows, pl.ds(2 * D_MIX + p * PAIR, PAIR)]
                loads.append((q_s[rows, cols], k_s[rows, cols], v, beta_s[rows, cols], g_s[rows, cols]))
        return loads

    s = [state[p] for p in range(N_PAIRS)]

    def sequential(chunks):
        for c in chunks:
            rows = pl.ds(c * CHUNK, CHUNK)
            ws_qs = [jnp.dot(wq_s[c * N_PAIRS + p], s[p].astype(BF16), preferred_element_type=F32)
                     for p in range(N_PAIRS)]
            yield
            v_new = [(u_s[c * N_PAIRS + p] - ws_qs[p][:CHUNK]).astype(BF16) for p in range(N_PAIRS)]
            e_last = jnp.exp(g_last[c])
            for p in range(N_PAIRS):
                head_decay = lambda h: jnp.broadcast_to(
                    e_last[:, N_HEADS + h:N_HEADS + h + 1], (HEAD_DIM, PAIR))
                s_decay = jnp.concatenate([head_decay(2 * p), head_decay(2 * p + 1)], axis=0)
                kv = jnp.dot(kdt_s[c * N_PAIRS + p], v_new[p], preferred_element_type=F32)
                s[p] = s[p] * s_decay + jnp.where(same_head, kv, 0.0)
            yield
            for p in range(N_PAIRS):
                o_s[rows, pl.ds(p * PAIR, PAIR)] = ws_qs[p][CHUNK:] + jnp.dot(
                    attn_s[c * N_PAIRS + p], _bd(v_new[p]), preferred_element_type=F32)
            yield

    half = n_chunks // 2
    first, second = range(half), range(half, n_chunks)
    scratch = (wq_s, u_s, attn_s, kdt_s)
    _drain(prepare(first))
    filler = prepare(second)
    _gdn_precompute(loads_of(first), masks, *scratch, filler=filler)
    _drain(filler)
    filler = sequential(first)
    _gdn_precompute(loads_of(second), masks, *scratch, base=half * N_PAIRS, filler=filler)
    _drain(filler)
    filler = finish(first)
    for _ in sequential(second):
        _advance(filler)
    _drain(filler)
    _drain(finish(second))
    for p in range(N_PAIRS):
        state[p] = s[p]


def _gdn(x_main, x_small, a_log_pad, dt_bias_pad, norm_g_full, consts):
    b, t, _ = x_main.shape
    tb = MIX_TB
    lblk, e128 = consts
    n_prob = (tb // CHUNK) * N_PAIRS
    const = lambda arr: pl.BlockSpec(arr.shape, lambda i, j: (0,) * arr.ndim)
    smalls = [a_log_pad, dt_bias_pad, norm_g_full, lblk, e128]
    return pl.pallas_call(
        _gdn_body,
        grid=(b, t // tb),
        in_specs=[pl.BlockSpec((1, tb, GDN_MAIN), lambda i, j: (i, j, 0)),
                  pl.BlockSpec((1, tb, GDN_SMALL), lambda i, j: (i, j, 0))] + [const(a) for a in smalls],
        out_specs=pl.BlockSpec((1, tb, D_MIX), lambda i, j: (i, j, 0)),
        out_shape=jax.ShapeDtypeStruct((b, t, D_MIX), F32),
        scratch_shapes=[pltpu.VMEM((tb, D_MIX), F32)] * 5
        + [pltpu.VMEM((N_PAIRS, PAIR, PAIR), F32),
           pltpu.VMEM((n_prob, 2 * CHUNK, PAIR), BF16),
           pltpu.VMEM((n_prob, CHUNK, PAIR), F32),
           pltpu.VMEM((n_prob, CHUNK, PAIR), BF16),
           pltpu.VMEM((n_prob, PAIR, CHUNK), BF16)],
        compiler_params=pltpu.CompilerParams(
            dimension_semantics=("parallel", "arbitrary"), vmem_limit_bytes=VMEM_LIMIT),
        name="gdn",
    )(x_main, x_small, *smalls)


def _rwkv_precompute(loads, masks, wr_s, ut_s, arb_s, y0_s, btt_s, z0_s, pc_s, base=0, filler=None):
    ti, tj, causal, strict, same_head = masks
    x_list, rest = [], []
    for r, k, v, aa, bb, lw, lp in loads:
        p_inv = jnp.exp(-lp)
        lp_last = lp[CHUNK - 1:CHUNK, :]
        p_rest = jnp.exp(lp_last - lp)
        a_h = (aa * jnp.exp(lp - lw)).astype(BF16)
        r_h = (r * jnp.exp(lp)).astype(BF16)
        gram = _dot_nt(jnp.concatenate([a_h, r_h], axis=0),
                       jnp.concatenate([_bd(bb * p_inv), _bd(k * p_inv)], axis=0))
        x_list.append(jnp.where(strict, -gram[:CHUNK, :PAIR], 0.0))
        a_ak = jnp.where(strict, gram[:CHUNK, PAIR:], 0.0)
        a_rb = jnp.where(causal, gram[CHUNK:, :PAIR], 0.0).astype(BF16)
        a_rk = jnp.where(causal, gram[CHUNK:, PAIR:], 0.0)
        av = _dot(jnp.concatenate([a_ak, a_rk], axis=0), _bd(v))
        btt = (bb * p_rest).T.astype(BF16)
        z0 = jnp.where(same_head, _dot_tn(k * p_rest, v), 0.0)
        pc = jnp.broadcast_to(jnp.exp(lp_last), (PAIR, PAIR)).T
        rest.append((a_h, r_h, a_rb, av, btt, z0, pc))
    ts = _tri_inverse_many(x_list, ti, tj, filler)
    for n, (t, (a_h, r_h, a_rb, av, btt, z0, pc)) in enumerate(zip(ts, rest), start=base):
        wu = jnp.dot(t, jnp.concatenate([_bd(a_h), _bd(av[:CHUNK])], axis=1),
                     preferred_element_type=F32)
        wr_s[n] = jnp.concatenate([wu[:, :PAIR].astype(BF16), r_h], axis=0)
        ut_s[n] = wu[:, PAIR:]
        arb_s[n] = a_rb
        y0_s[n] = av[CHUNK:]
        btt_s[n] = btt
        z0_s[n] = z0
        pc_s[n] = pc


def _rwkv_body(p_ref, w0_ref, a0_ref, w2_ref, a2_ref, g2_ref, kk_ref, ka_ref, rk_ref,
               lng_ref, lnb_ref, lblk_ref, e128_ref, o_ref,
               k_s, aa_s, bb_s, lw_s, lp_s, gate_s, y_s, state,
               wr_s, ut_s, arb_s, y0_s, btt_s, z0_s, pc_s):
    tb = MIX_TB

    @pl.when(pl.program_id(1) == 0)
    def _():
        state[...] = jnp.zeros_like(state)

    v_cols = 2 * D_MIX
    e128 = e128_ref[...]
    masks = _pair_masks()
    same_head = masks[-1]
    n_chunks = tb // CHUNK

    def prepare(chunks):
        slab = lblk_ref.shape[0]
        for r0 in range(chunks[0] * CHUNK, (chunks[-1] + 1) * CHUNK, slab):
            rows = pl.ds(r0, slab)
            k = p_ref[0, rows, D_MIX:2 * D_MIX]
            lora_wa = p_ref[0, rows, 3 * D_MIX:3 * D_MIX + LORA_W + LORA_A]
            g_lo = p_ref[0, rows, 3 * D_MIX + LORA_W + LORA_A:]
            w = -_softplus(-(w0_ref[...] + _dot(jnp.tanh(lora_wa), w2_ref[...]))) - 0.5
            lw = -jnp.exp(w)
            a = _sigmoid(a0_ref[...] + _dot(lora_wa, a2_ref[...]))
            gate_s[rows, :] = _dot(_sigmoid(g_lo), g2_ref[...])
            kk = k * kk_ref[...]
            kk = kk * lax.rsqrt(_head_sums(kk * kk, e128) + L2_EPS)
            k_s[rows, :] = k * (1.0 + (a - 1.0) * ka_ref[...])
            aa_s[rows, :] = -kk
            bb_s[rows, :] = kk * a
            lw_s[rows, :] = lw
            lp_s[rows, :] = _sel_dot(lblk_ref[...], lw)
            yield

    def finish(chunks):
        inv_d = 1.0 / HEAD_DIM
        for c in chunks:
            rows = pl.ds(c * CHUNK, CHUNK)
            y = y_s[rows, :]
            yc = y - _head_sums(y, e128) * inv_d
            y = yc * lax.rsqrt(_head_sums(yc * yc, e128) * inv_d + GN_EPS) * lng_ref[...] + lnb_ref[...]
            r = p_ref[0, rows, 0:D_MIX]
            bonus = _head_sums(r * k_s[rows, :] * rk_ref[...], e128) * p_ref[0, rows, v_cols:v_cols + D_MIX]
            o_ref[0, rows, :] = (y + bonus) * gate_s[rows, :]
            yield

    def loads_of(chunks):
        loads = []
        for c in chunks:
            rows = pl.ds(c * CHUNK, CHUNK)
            for pr in range(N_PAIRS):
                cols = pl.ds(pr * PAIR, PAIR)
                r = p_ref[0, rows, cols]
                v = p_ref[0, rows, pl.ds(v_cols + pr * PAIR, PAIR)]
                loads.append((r, k_s[rows, cols], v, aa_s[rows, cols],
                              bb_s[rows, cols], lw_s[rows, cols], lp_s[rows, cols]))
        return loads

    hs = [state[pr] for pr in range(N_PAIRS)]

    def sequential(chunks):
        for c in chunks:
            rows = pl.ds(c * CHUNK, CHUNK)
            wh_rh = [jnp.dot(wr_s[c * N_PAIRS + pr], hs[pr].astype(BF16), preferred_element_type=F32)
                     for pr in range(N_PAIRS)]
            yield
            u = [(wh_rh[pr][:CHUNK] + ut_s[c * N_PAIRS + pr]).astype(BF16) for pr in range(N_PAIRS)]
            for pr in range(N_PAIRS):
                n = c * N_PAIRS + pr
                bu = jnp.dot(btt_s[n], u[pr], preferred_element_type=F32)
                hs[pr] = pc_s[n] * hs[pr] + jnp.where(same_head, bu, 0.0) + z0_s[n]
            yield
            for pr in range(N_PAIRS):
                n = c * N_PAIRS + pr
                y_s[rows, pl.ds(pr * PAIR, PAIR)] = (
                    wh_rh[pr][CHUNK:] + jnp.dot(arb_s[n], _bd(u[pr]), preferred_element_type=F32)
                    + y0_s[n])
            yield

    half = n_chunks // 2
    first, second = range(half), range(half, n_chunks)
    scratch = (wr_s, ut_s, arb_s, y0_s, btt_s, z0_s, pc_s)
    _drain(prepare(first))
    filler = prepare(second)
    _rwkv_precompute(loads_of(first), masks, *scratch, filler=filler)
    _drain(filler)
    filler = sequential(first)
    _rwkv_precompute(loads_of(second), masks, *scratch, base=half * N_PAIRS, filler=filler)
    _drain(filler)
    filler = finish(first)
    for _ in sequential(second):
        _advance(filler)
    _drain(filler)
    _drain(finish(second))
    for pr in range(N_PAIRS):
        state[pr] = hs[pr]


def _rwkv(p, params, consts):
    b, t, _ = p.shape
    tb = MIX_TB
    lblk, e128 = consts
    n_prob = (tb // CHUNK) * N_PAIRS
    const = lambda arr: pl.BlockSpec(arr.shape, lambda i, j: (0,) * arr.ndim)
    smalls = list(params) + [lblk, e128]
    return pl.pallas_call(
        _rwkv_body,
        grid=(b, t // tb),
        in_specs=[pl.BlockSpec((1, tb, RWKV_COLS), lambda i, j: (i, j, 0))] + [const(a) for a in smalls],
        out_specs=pl.BlockSpec((1, tb, D_MIX), lambda i, j: (i, j, 0)),
        out_shape=jax.ShapeDtypeStruct((b, t, D_MIX), F32),
        scratch_shapes=[pltpu.VMEM((tb, D_MIX), F32)] * 7
        + [pltpu.VMEM((N_PAIRS, PAIR, PAIR), F32),
           pltpu.VMEM((n_prob, 2 * CHUNK, PAIR), BF16),
           pltpu.VMEM((n_prob, CHUNK, PAIR), F32),
           pltpu.VMEM((n_prob, CHUNK, PAIR), BF16),
           pltpu.VMEM((n_prob, CHUNK, PAIR), F32),
           pltpu.VMEM((n_prob, PAIR, CHUNK), BF16),
           pltpu.VMEM((n_prob, PAIR, PAIR), F32),
           pltpu.VMEM((n_prob, PAIR, PAIR), F32)],
        compiler_params=pltpu.CompilerParams(
            dimension_semantics=("parallel", "arbitrary"), vmem_limit_bytes=VMEM_LIMIT),
        name="rwkv",
    )(p, *smalls)


def _selection_constants(tb):
    i = jnp.arange(tb)
    lblk = ((i[:, None] // CHUNK == i[None, :] // CHUNK) & (i[:, None] >= i[None, :])).astype(BF16)
    c = jnp.arange(PAIR)
    e128 = (c[:, None] // HEAD_DIM == c[None, :] // HEAD_DIM).astype(BF16)
    return lblk, e128


def kernel(x, ffn1_pre_g, ffn1_w_gate, ffn1_w_up, ffn1_w_down, ffn1_post_g, mix_pre_g, w_in, gdn_conv_w, gdn_a_log, gdn_dt_bias, gdn_norm_g, rwkv_mu, rwkv_w0, rwkv_w2, rwkv_a0, rwkv_a2, rwkv_g2, rwkv_k_k, rwkv_k_a, rwkv_r_k, rwkv_ln_g, rwkv_ln_b, w_out, mix_post_g, ffn2_pre_g, ffn2_w_gate, ffn2_w_up, ffn2_w_down, ffn2_post_g):
    b, t, d = x.shape
    depth = ffn1_pre_g.shape[0]
    consts = _selection_constants(CHUNK)
    rwkv_consts = _selection_constants(RWKV_PREP_ROWS)
    row = lambda v: v.reshape(1, -1).astype(F32)
    h = x.reshape(b * t, d)
    for l in range(depth):
        h = _ffn(h, row(ffn1_pre_g[l]), ffn1_w_gate[l].astype(BF16), ffn1_w_up[l].astype(BF16),
                 ffn1_w_down[l].astype(BF16), row(ffn1_post_g[l]))

        wl = w_in[l]
        n_qkvz = 4 * D_MIX
        w_cat = jnp.concatenate(
            [wl[:, :n_qkvz], wl[:, n_qkvz:n_qkvz + 2 * N_HEADS],
             jnp.zeros((d, GDN_SMALL - 2 * N_HEADS), wl.dtype), wl[:, n_qkvz + 2 * N_HEADS:]], axis=1)
        x_main, x_small, x_rwkv = _in_proj(h, row(mix_pre_g[l]), w_cat.astype(BF16),
                                           gdn_conv_w[l].astype(F32), row(rwkv_mu[l]), t)

        pad_small = lambda v: jnp.zeros((1, GDN_SMALL), F32).at[0, N_HEADS:2 * N_HEADS].set(v)
        y_gdn = _gdn(x_main.reshape(b, t, GDN_MAIN), x_small.reshape(b, t, GDN_SMALL),
                     pad_small(gdn_a_log[l]), pad_small(gdn_dt_bias[l]),
                     row(jnp.tile(gdn_norm_g[l], N_HEADS)), consts)

        zeros_lora = jnp.zeros((LORA_W, D_MIX), F32)
        w2_pad = jnp.concatenate([rwkv_w2[l], zeros_lora], axis=0).astype(BF16)
        a2_pad = jnp.concatenate([zeros_lora, rwkv_a2[l]], axis=0).astype(BF16)
        rwkv_params = (row(rwkv_w0[l]), row(rwkv_a0[l]), w2_pad, a2_pad,
                       rwkv_g2[l].astype(BF16), row(rwkv_k_k[l]), row(rwkv_k_a[l]), row(rwkv_r_k[l]),
                       row(rwkv_ln_g[l]), row(rwkv_ln_b[l]))
        y_rwkv = _rwkv(x_rwkv.reshape(b, t, RWKV_COLS), rwkv_params, rwkv_consts)

        h = _ffn(h, row(ffn2_pre_g[l]), ffn2_w_gate[l].astype(BF16), ffn2_w_up[l].astype(BF16),
                 ffn2_w_down[l].astype(BF16), row(ffn2_post_g[l]),
                 mix=(y_gdn.reshape(b * t, D_MIX), y_rwkv.reshape(b * t, D_MIX),
                      w_out[l].astype(BF16), row(mix_post_g[l])))
    return h.reshape(b, t, d)
```

```python
import functools

import jax
import jax.numpy as jnp
from jax import lax
from jax.experimental import pallas as pl
from jax.experimental.pallas import tpu as pltpu

F32 = jnp.float32
BF16 = jnp.bfloat16

D_MODEL = 1024
D_FF = 2816
HEAD_DIM = 64
N_HEADS = 8
D_MIX = N_HEADS * HEAD_DIM
N_PAIRS = N_HEADS // 2
PAIR = 2 * HEAD_DIM
CHUNK = 64
CONV_WIDTH = 4
LORA_W = 64
LORA_A = 64
LORA_G = 128
EPS = 1e-6
L2_EPS = 1e-6
GN_EPS = 64e-5

LANES = 128
SUBLANES = 8
GDN_MAIN = 4 * D_MIX
GDN_SMALL = LANES
RWKV_COLS = 3 * D_MIX + LORA_W + LORA_A + LORA_G

FFN_TM = 1024
FFN_PART = 256
PROJ_TM = 512
MIX_TB = 512
RWKV_PREP_ROWS = 256
VMEM_LIMIT = 56 * 1024 * 1024


def _dot(a, b):
    return jnp.dot(a.astype(BF16), b.astype(BF16), preferred_element_type=F32)


def _dot_nt(a, b):
    return lax.dot_general(a.astype(BF16), b.astype(BF16), (((1,), (1,)), ((), ())),
                           preferred_element_type=F32)


def _dot_tn(a, b):
    return lax.dot_general(a.astype(BF16), b.astype(BF16), (((0,), (0,)), ((), ())),
                           preferred_element_type=F32)


def _split(x, passes):
    pieces = []
    rem = x
    for i in range(passes):
        p = rem.astype(BF16)
        pieces.append(p)
        if i + 1 < passes:
            rem = rem - p.astype(F32)
    return pieces


def _head_sums(x, e128):
    xb = x.astype(BF16)
    return jnp.concatenate(
        [jnp.dot(xb[:, g * PAIR:(g + 1) * PAIR], e128, preferred_element_type=F32)
         for g in range(N_PAIRS)], axis=1)


def _sel_dot(sel, x, passes=3):
    out = None
    for p in _split(x, passes):
        t = jnp.dot(sel, p, preferred_element_type=F32)
        out = t if out is None else out + t
    return out


def _rmsnorm(x, g):
    return x * lax.rsqrt(jnp.mean(x * x, axis=-1, keepdims=True) + EPS) * g


def _sigmoid(x):
    return jax.nn.sigmoid(x)


def _silu(x):
    hx = 0.5 * x
    return hx + hx * jnp.tanh(hx)


def _softplus(x):
    return jnp.maximum(x, 0.0) + jnp.log(1.0 + jnp.exp(-jnp.abs(x)))


def _bd(x):
    x = x.astype(BF16)
    lane = lax.broadcasted_iota(jnp.int32, x.shape, 1)
    zero = jnp.zeros_like(x)
    return jnp.concatenate([jnp.where(lane < HEAD_DIM, x, zero),
                            jnp.where(lane >= HEAD_DIM, x, zero)], axis=0)


def _advance(filler):
    if filler is not None:
        next(filler, None)


def _drain(stages):
    for _ in stages:
        pass


def _skewed(parts):
    live = []
    pending = list(parts)
    while pending or live:
        if pending:
            live.append(pending.pop(0))
        for g in list(live):
            if next(g, StopIteration) is StopIteration:
                live.remove(g)


def _tri_inverse_many(xs, ti, tj, filler=None):
    def sub_blocks(b):
        return ((ti ^ tj) < 2 * b) & ((ti & b) != 0) & ((tj & b) == 0)

    eye = jnp.where(ti == tj, 1.0, 0.0)
    ts = [(eye - jnp.where(sub_blocks(1), x, 0.0)).astype(BF16) for x in xs]
    xbs = [x.astype(BF16) for x in xs]
    zero = jnp.zeros((CHUNK, PAIR), BF16)
    b = 2
    while b < CHUNK:
        m = sub_blocks(b)
        tls = [jnp.dot(t, _bd(jnp.where(m, xb, zero)), preferred_element_type=F32)
               for t, xb in zip(ts, xbs)]
        _advance(filler)
        ts = [jnp.dot((eye - tl).astype(BF16), _bd(t), preferred_element_type=F32).astype(BF16)
              for tl, t in zip(tls, ts)]
        _advance(filler)
        b *= 2
    return ts


def _pair_masks():
    ti = lax.broadcasted_iota(jnp.int32, (CHUNK, PAIR), 0)
    tj = lax.broadcasted_iota(jnp.int32, (CHUNK, PAIR), 1) & (HEAD_DIM - 1)
    ii = lax.broadcasted_iota(jnp.int32, (PAIR, PAIR), 0)
    jj = lax.broadcasted_iota(jnp.int32, (PAIR, PAIR), 1)
    return ti, tj, ti >= tj, ti > tj, (ii ^ jj) < HEAD_DIM


def _ffn_body(with_mix, *refs):
    if with_mix:
        (yg_ref, yr_ref, h_ref, wout_ref, mixg_ref, preg_ref, wg_ref, wu_ref, wd_ref, postg_ref,
         o_ref) = refs
    else:
        (h_ref, preg_ref, wg_ref, wu_ref, wd_ref, postg_ref, o_ref) = refs

    def half(rows):
        h = h_ref[rows, :]
        if with_mix:
            y = jnp.concatenate([yg_ref[rows, :], yr_ref[rows, :]], axis=-1)
            h = h + _rmsnorm(_dot(y, wout_ref[...]), mixg_ref[...])
        xn = _rmsnorm(h, preg_ref[...]).astype(BF16)
        yield
        gate = jnp.dot(xn, wg_ref[...], preferred_element_type=F32)
        up = jnp.dot(xn, wu_ref[...], preferred_element_type=F32)
        yield
        hid = (_silu(gate) * up).astype(BF16)
        f = jnp.dot(hid, wd_ref[...], preferred_element_type=F32)
        yield
        o_ref[rows, :] = h + 0.5 * _rmsnorm(f, postg_ref[...])
        yield

    _skewed([half(pl.ds(i * FFN_PART, FFN_PART)) for i in range(FFN_TM // FFN_PART)])


def _ffn(h, pre_g, w_gate, w_up, w_down, post_g, mix=None):
    m = h.shape[0]
    row = lambda width: pl.BlockSpec((FFN_TM, width), lambda i: (i, 0))
    resident = pl.BlockSpec(memory_space=pltpu.VMEM)
    in_specs, args = [], []
    if mix is not None:
        y_gdn, y_rwkv, w_out, mix_g = mix
        in_specs += [row(D_MIX), row(D_MIX)]
        args += [y_gdn, y_rwkv]
    in_specs.append(row(D_MODEL))
    args.append(h)
    if mix is not None:
        in_specs += [resident, resident]
        args += [w_out, mix_g]
    in_specs += [resident] * 5
    args += [pre_g, w_gate, w_up, w_down, post_g]
    return pl.pallas_call(
        functools.partial(_ffn_body, mix is not None),
        grid=(m // FFN_TM,),
        in_specs=in_specs,
        out_specs=row(D_MODEL),
        out_shape=jax.ShapeDtypeStruct((m, D_MODEL), F32),
        compiler_params=pltpu.CompilerParams(
            dimension_semantics=("parallel",), vmem_limit_bytes=VMEM_LIMIT),
        name="ffn_mix" if mix is not None else "ffn",
    )(*args)


def _proj_body(tiles_per_seq, h_ref, g_ref, w_ref, convw_ref, mu_ref, main_ref, small_ref, rwkv_ref,
               qkv_buf, rwkv_buf):
    tm = PROJ_TM

    @pl.when(pl.program_id(0) % tiles_per_seq == 0)
    def _():
        qkv_buf[0:SUBLANES, :] = jnp.zeros((SUBLANES, 3 * D_MIX), F32)
        rwkv_buf[0:SUBLANES, :] = jnp.zeros((SUBLANES, RWKV_COLS), F32)

    def half(r0, n):
        rows = pl.ds(r0, n)
        stored = pl.ds(SUBLANES + r0, n)
        window = pl.ds(r0, SUBLANES + n)
        xn = _rmsnorm(h_ref[rows, :], g_ref[...]).astype(BF16)
        proj = lambda lo, hi: jnp.dot(xn, w_ref[:, lo:hi], preferred_element_type=F32)
        yield
        rwkv_buf[stored, :] = proj(GDN_MAIN + GDN_SMALL, GDN_MAIN + GDN_SMALL + RWKV_COLS)
        qkv_buf[stored, :] = proj(0, 3 * D_MIX)
        yield
        pall = rwkv_buf[window, :]
        cur = pall[SUBLANES:]
        prev = pltpu.roll(pall, 1, axis=0)[SUBLANES:]
        rwkv_ref[rows, :] = cur + (prev - cur) * mu_ref[...]
        xall = qkv_buf[window, :]
        conv = xall[SUBLANES:] * convw_ref[CONV_WIDTH - 1:CONV_WIDTH, :]
        for d in range(1, CONV_WIDTH):
            shifted = pltpu.roll(xall, d, axis=0)[SUBLANES:]
            conv = conv + shifted * convw_ref[CONV_WIDTH - 1 - d:CONV_WIDTH - d, :]
        main_ref[rows, 0:3 * D_MIX] = _silu(conv)
        main_ref[rows, 3 * D_MIX:] = proj(3 * D_MIX, GDN_MAIN)
        small_ref[rows, :] = proj(GDN_MAIN, GDN_MAIN + GDN_SMALL)
        yield

    _skewed([half(0, tm // 2), half(tm // 2, tm // 2)])

    qkv_buf[0:SUBLANES, :] = qkv_buf[tm:tm + SUBLANES, :]
    rwkv_buf[0:SUBLANES, :] = rwkv_buf[tm:tm + SUBLANES, :]


def _in_proj(h, g, w, conv_w, mu, seq_len):
    m = h.shape[0]
    n = w.shape[1]
    row = lambda width: pl.BlockSpec((PROJ_TM, width), lambda i: (i, 0))
    const = lambda arr: pl.BlockSpec(arr.shape, lambda i: (0, 0))
    return pl.pallas_call(
        functools.partial(_proj_body, seq_len // PROJ_TM),
        grid=(m // PROJ_TM,),
        in_specs=[row(D_MODEL), const(g), pl.BlockSpec((D_MODEL, n), lambda i: (0, 0)),
                  const(conv_w), const(mu)],
        out_specs=[row(GDN_MAIN), row(GDN_SMALL), row(RWKV_COLS)],
        out_shape=[jax.ShapeDtypeStruct((m, GDN_MAIN), F32),
                   jax.ShapeDtypeStruct((m, GDN_SMALL), F32),
                   jax.ShapeDtypeStruct((m, RWKV_COLS), F32)],
        scratch_shapes=[pltpu.VMEM((PROJ_TM + SUBLANES, 3 * D_MIX), F32),
                        pltpu.VMEM((PROJ_TM + SUBLANES, RWKV_COLS), F32)],
        compiler_params=pltpu.CompilerParams(
            dimension_semantics=("arbitrary",), vmem_limit_bytes=VMEM_LIMIT),
        name="in_proj",
    )(h, g, w, conv_w, mu)


def _gdn_precompute(loads, masks, wq_s, u_s, attn_s, kdt_s, base=0, filler=None):
    ti, tj, causal, strict, _ = masks
    a_list, rest = [], []
    for q, k, v, beta, g_nat in loads:
        g_last = g_nat[CHUNK - 1:CHUNK, :]
        e_g = jnp.exp(g_nat)
        kb = k * beta
        g_row = jnp.sum(jnp.where(ti == tj, g_nat, 0.0), axis=0, keepdims=True)
        decay = jnp.exp(jnp.where(causal, g_nat - g_row, -jnp.inf))
        gram = _dot_nt(jnp.concatenate([kb, q], axis=0), _bd(k))
        a_list.append(jnp.where(strict, gram[:CHUNK] * decay, 0.0))
        attn = (gram[CHUNK:] * decay).astype(BF16)
        rhs = jnp.concatenate([_bd(v * beta), _bd(kb * e_g)], axis=1)
        kdt = (k * jnp.exp(g_last - g_nat)).T.astype(BF16)
        rest.append((attn, rhs, (q * e_g).astype(BF16), kdt))
    ts = _tri_inverse_many(a_list, ti, tj, filler)
    for n, (t, (attn, rhs, qg, kdt)) in enumerate(zip(ts, rest), start=base):
        uw = jnp.dot(t, rhs, preferred_element_type=F32)
        wq_s[n] = jnp.concatenate([uw[:, PAIR:].astype(BF16), qg], axis=0)
        u_s[n] = uw[:, :PAIR]
        attn_s[n] = attn
        kdt_s[n] = kdt


def _gdn_body(xm_ref, xs_ref, alog_ref, dtb_ref, ng_ref, lblk_ref, e128_ref,
              o_ref, q_s, k_s, beta_s, g_s, o_s, state,
              wq_s, u_s, attn_s, kdt_s):
    tb = MIX_TB

    @pl.when(pl.program_id(1) == 0)
    def _():
        state[...] = jnp.zeros_like(state)

    e128 = e128_ref[...]
    masks = _pair_masks()
    same_head = masks[-1]
    n_chunks = tb // CHUNK
    first_head = lax.broadcasted_iota(jnp.int32, (CHUNK, LANES), 1) < HEAD_DIM
    lane_bcast = lambda x, j: jnp.broadcast_to(x[:, j:j + 1], (CHUNK, LANES))
    g_last = {}

    def prepare(chunks):
        for c in chunks:
            rows = pl.ds(c * CHUNK, CHUNK)
            q = xm_ref[0, rows, 0:D_MIX]
            k = xm_ref[0, rows, D_MIX:2 * D_MIX]
            q_s[rows, :] = q * lax.rsqrt(_head_sums(q * q, e128) + L2_EPS) * (HEAD_DIM ** -0.5)
            k_s[rows, :] = k * lax.rsqrt(_head_sums(k * k, e128) + L2_EPS)
            xs = xs_ref[0, rows, :]
            beta = _sigmoid(xs)
            g_small = -jnp.exp(alog_ref[...]) * _softplus(xs + dtb_ref[...])
            g_cum = _sel_dot(lblk_ref[...], g_small)
            g_last[c] = g_cum[CHUNK - 1:CHUNK, :]
            for p in range(N_PAIRS):
                cols = pl.ds(p * PAIR, PAIR)
                g_s[rows, cols] = jnp.where(first_head, lane_bcast(g_cum, N_HEADS + 2 * p),
                                            lane_bcast(g_cum, N_HEADS + 2 * p + 1))
                beta_s[rows, cols] = jnp.where(first_head, lane_bcast(beta, 2 * p),
                                               lane_bcast(beta, 2 * p + 1))
            yield

    def finish(chunks):
        for c in chunks:
            rows = pl.ds(c * CHUNK, CHUNK)
            o = o_s[rows, :]
            z = xm_ref[0, rows, 3 * D_MIX:4 * D_MIX]
            ms = _head_sums(o * o, e128) * (1.0 / HEAD_DIM)
            o_ref[0, rows, :] = o * lax.rsqrt(ms + EPS) * ng_ref[...] * _silu(z)
            yield

    def loads_of(chunks):
        loads = []
        for c in chunks:
            rows = pl.ds(c * CHUNK, CHUNK)
            for p in range(N_PAIRS):
                cols = pl.ds(p * PAIR, PAIR)
                v = xm_ref[0, rows, pl.ds(2 * D_MIX + p * PAIR, PAIR)]
                loads.append((q_s[rows, cols], k_s[rows, cols], v, beta_s[rows, cols], g_s[rows, cols]))
        return loads

    s = [state[p] for p in range(N_PAIRS)]

    def sequential(chunks):
        for c in chunks:
            rows = pl.ds(c * CHUNK, CHUNK)
            ws_qs = [jnp.dot(wq_s[c * N_PAIRS + p], s[p].astype(BF16), preferred_element_type=F32)
                     for p in range(N_PAIRS)]
            yield
            v_new = [(u_s[c * N_PAIRS + p] - ws_qs[p][:CHUNK]).astype(BF16) for p in range(N_PAIRS)]
            e_last = jnp.exp(g_last[c])
            for p in range(N_PAIRS):
                head_decay = lambda h: jnp.broadcast_to(
                    e_last[:, N_HEADS + h:N_HEADS + h + 1], (HEAD_DIM, PAIR))
                s_decay = jnp.concatenate([head_decay(2 * p), head_decay(2 * p + 1)], axis=0)
                kv = jnp.dot(kdt_s[c * N_PAIRS + p], v_new[p], preferred_element_type=F32)
                s[p] = s[p] * s_decay + jnp.where(same_head, kv, 0.0)
            yield
            for p in range(N_PAIRS):
                o_s[rows, pl.ds(p * PAIR, PAIR)] = ws_qs[p][CHUNK:] + jnp.dot(
                    attn_s[c * N_PAIRS + p], _bd(v_new[p]), preferred_element_type=F32)
            yield

    half = n_chunks // 2
    first, second = range(half), range(half, n_chunks)
    scratch = (wq_s, u_s, attn_s, kdt_s)
    _drain(prepare(first))
    filler = prepare(second)
    _gdn_precompute(loads_of(first), masks, *scratch, filler=filler)
    _drain(filler)
    filler = sequential(first)
    _gdn_precompute(loads_of(second), masks, *scratch, base=half * N_PAIRS, filler=filler)
    _drain(filler)
    filler = finish(first)
    for _ in sequential(second):
        _advance(filler)
    _drain(filler)
    _drain(finish(second))
    for p in range(N_PAIRS):
        state[p] = s[p]


def _gdn(x_main, x_small, a_log_pad, dt_bias_pad, norm_g_full, consts):
    b, t, _ = x_main.shape
    tb = MIX_TB
    lblk, e128 = consts
    n_prob = (tb // CHUNK) * N_PAIRS
    const = lambda arr: pl.BlockSpec(arr.shape, lambda i, j: (0,) * arr.ndim)
    smalls = [a_log_pad, dt_bias_pad, norm_g_full, lblk, e128]
    return pl.pallas_call(
        _gdn_body,
        grid=(b, t // tb),
        in_specs=[pl.BlockSpec((1, tb, GDN_MAIN), lambda i, j: (i, j, 0)),
                  pl.BlockSpec((1, tb, GDN_SMALL), lambda i, j: (i, j, 0))] + [const(a) for a in smalls],
        out_specs=pl.BlockSpec((1, tb, D_MIX), lambda i, j: (i, j, 0)),
        out_shape=jax.ShapeDtypeStruct((b, t, D_MIX), F32),
        scratch_shapes=[pltpu.VMEM((tb, D_MIX), F32)] * 5
        + [pltpu.VMEM((N_PAIRS, PAIR, PAIR), F32),
           pltpu.VMEM((n_prob, 2 * CHUNK, PAIR), BF16),
           pltpu.VMEM((n_prob, CHUNK, PAIR), F32),
           pltpu.VMEM((n_prob, CHUNK, PAIR), BF16),
           pltpu.VMEM((n_prob, PAIR, CHUNK), BF16)],
        compiler_params=pltpu.CompilerParams(
            dimension_semantics=("parallel", "arbitrary"), vmem_limit_bytes=VMEM_LIMIT),
        name="gdn",
    )(x_main, x_small, *smalls)


def _rwkv_precompute(loads, masks, wr_s, ut_s, arb_s, y0_s, btt_s, z0_s, pc_s, base=0, filler=None):
    ti, tj, causal, strict, same_head = masks
    x_list, rest = [], []
    for r, k, v, aa, bb, lw, lp in loads:
        p_inv = jnp.exp(-lp)
        lp_last = lp[CHUNK - 1:CHUNK, :]
        p_rest = jnp.exp(lp_last - lp)
        a_h = (aa * jnp.exp(lp - lw)).astype(BF16)
        r_h = (r * jnp.exp(lp)).astype(BF16)
        gram = _dot_nt(jnp.concatenate([a_h, r_h], axis=0),
                       jnp.concatenate([_bd(bb * p_inv), _bd(k * p_inv)], axis=0))
        x_list.append(jnp.where(strict, -gram[:CHUNK, :PAIR], 0.0))
        a_ak = jnp.where(strict, gram[:CHUNK, PAIR:], 0.0)
        a_rb = jnp.where(causal, gram[CHUNK:, :PAIR], 0.0).astype(BF16)
        a_rk = jnp.where(causal, gram[CHUNK:, PAIR:], 0.0)
        av = _dot(jnp.concatenate([a_ak, a_rk], axis=0), _bd(v))
        btt = (bb * p_rest).T.astype(BF16)
        z0 = jnp.where(same_head, _dot_tn(k * p_rest, v), 0.0)
        pc = jnp.broadcast_to(jnp.exp(lp_last), (PAIR, PAIR)).T
        rest.append((a_h, r_h, a_rb, av, btt, z0, pc))
    ts = _tri_inverse_many(x_list, ti, tj, filler)
    for n, (t, (a_h, r_h, a_rb, av, btt, z0, pc)) in enumerate(zip(ts, rest), start=base):
        wu = jnp.dot(t, jnp.concatenate([_bd(a_h), _bd(av[:CHUNK])], axis=1),
                     preferred_element_type=F32)
        wr_s[n] = jnp.concatenate([wu[:, :PAIR].astype(BF16), r_h], axis=0)
        ut_s[n] = wu[:, PAIR:]
        arb_s[n] = a_rb
        y0_s[n] = av[CHUNK:]
        btt_s[n] = btt
        z0_s[n] = z0
        pc_s[n] = pc


def _rwkv_body(p_ref, w0_ref, a0_ref, w2_ref, a2_ref, g2_ref, kk_ref, ka_ref, rk_ref,
               lng_ref, lnb_ref, lblk_ref, e128_ref, o_ref,
               k_s, aa_s, bb_s, lw_s, lp_s, gate_s, y_s, state,
               wr_s, ut_s, arb_s, y0_s, btt_s, z0_s, pc_s):
    tb = MIX_TB

    @pl.when(pl.program_id(1) == 0)
    def _():
        state[...] = jnp.zeros_like(state)

    v_cols = 2 * D_MIX
    e128 = e128_ref[...]
    masks = _pair_masks()
    same_head = masks[-1]
    n_chunks = tb // CHUNK

    def prepare(chunks):
        slab = lblk_ref.shape[0]
        for r0 in range(chunks[0] * CHUNK, (chunks[-1] + 1) * CHUNK, slab):
            rows = pl.ds(r0, slab)
            k = p_ref[0, rows, D_MIX:2 * D_MIX]
            lora_wa = p_ref[0, rows, 3 * D_MIX:3 * D_MIX + LORA_W + LORA_A]
            g_lo = p_ref[0, rows, 3 * D_MIX + LORA_W + LORA_A:]
            w = -_softplus(-(w0_ref[...] + _dot(jnp.tanh(lora_wa), w2_ref[...]))) - 0.5
            lw = -jnp.exp(w)
            a = _sigmoid(a0_ref[...] + _dot(lora_wa, a2_ref[...]))
            gate_s[rows, :] = _dot(_sigmoid(g_lo), g2_ref[...])
            kk = k * kk_ref[...]
            kk = kk * lax.rsqrt(_head_sums(kk * kk, e128) + L2_EPS)
            k_s[rows, :] = k * (1.0 + (a - 1.0) * ka_ref[...])
            aa_s[rows, :] = -kk
            bb_s[rows, :] = kk * a
            lw_s[rows, :] = lw
            lp_s[rows, :] = _sel_dot(lblk_ref[...], lw)
            yield

    def finish(chunks):
        inv_d = 1.0 / HEAD_DIM
        for c in chunks:
            rows = pl.ds(c * CHUNK, CHUNK)
            y = y_s[rows, :]
            yc = y - _head_sums(y, e128) * inv_d
            y = yc * lax.rsqrt(_head_sums(yc * yc, e128) * inv_d + GN_EPS) * lng_ref[...] + lnb_ref[...]
            r = p_ref[0, rows, 0:D_MIX]
            bonus = _head_sums(r * k_s[rows, :] * rk_ref[...], e128) * p_ref[0, rows, v_cols:v_cols + D_MIX]
            o_ref[0, rows, :] = (y + bonus) * gate_s[rows, :]
            yield

    def loads_of(chunks):
        loads = []
        for c in chunks:
            rows = pl.ds(c * CHUNK, CHUNK)
            for pr in range(N_PAIRS):
                cols = pl.ds(pr * PAIR, PAIR)
                r = p_ref[0, rows, cols]
                v = p_ref[0, rows, pl.ds(v_cols + pr * PAIR, PAIR)]
                loads.append((r, k_s[rows, cols], v, aa_s[rows, cols],
                              bb_s[rows, cols], lw_s[rows, cols], lp_s[rows, cols]))
        return loads

    hs = [state[pr] for pr in range(N_PAIRS)]

    def sequential(chunks):
        for c in chunks:
            rows = pl.ds(c * CHUNK, CHUNK)
            wh_rh = [jnp.dot(wr_s[c * N_PAIRS + pr], hs[pr].astype(BF16), preferred_element_type=F32)
                     for pr in range(N_PAIRS)]
            yield
            u = [(wh_rh[pr][:CHUNK] + ut_s[c * N_PAIRS + pr]).astype(BF16) for pr in range(N_PAIRS)]
            for pr in range(N_PAIRS):
                n = c * N_PAIRS + pr
                bu = jnp.dot(btt_s[n], u[pr], preferred_element_type=F32)
                hs[pr] = pc_s[n] * hs[pr] + jnp.where(same_head, bu, 0.0) + z0_s[n]
            yield
            for pr in range(N_PAIRS):
                n = c * N_PAIRS + pr
                y_s[rows, pl.ds(pr * PAIR, PAIR)] = (
                    wh_rh[pr][CHUNK:] + jnp.dot(arb_s[n], _bd(u[pr]), preferred_element_type=F32)
                    + y0_s[n])
            yield

    half = n_chunks // 2
    first, second = range(half), range(half, n_chunks)
    scratch = (wr_s, ut_s, arb_s, y0_s, btt_s, z0_s, pc_s)
    _drain(prepare(first))
    filler = prepare(second)
    _rwkv_precompute(loads_of(first), masks, *scratch, filler=filler)
    _drain(filler)
    filler = sequential(first)
    _rwkv_precompute(loads_of(second), masks, *scratch, base=half * N_PAIRS, filler=filler)
    _drain(filler)
    filler = finish(first)
    for _ in sequential(second):
        _advance(filler)
    _drain(filler)
    _drain(finish(second))
    for pr in range(N_PAIRS):
        state[pr] = hs[pr]


def _rwkv(p, params, consts):
    b, t, _ = p.shape
    tb = MIX_TB
    lblk, e128 = consts
    n_prob = (tb // CHUNK) * N_PAIRS
    const = lambda arr: pl.BlockSpec(arr.shape, lambda i, j: (0,) * arr.ndim)
    smalls = list(params) + [lblk, e128]
    return pl.pallas_call(
        _rwkv_body,
        grid=(b, t // tb),
        in_specs=[pl.BlockSpec((1, tb, RWKV_COLS), lambda i, j: (i, j, 0))] + [const(a) for a in smalls],
        out_specs=pl.BlockSpec((1, tb, D_MIX), lambda i, j: (i, j, 0)),
        out_shape=jax.ShapeDtypeStruct((b, t, D_MIX), F32),
        scratch_shapes=[pltpu.VMEM((tb, D_MIX), F32)] * 7
        + [pltpu.VMEM((N_PAIRS, PAIR, PAIR), F32),
           pltpu.VMEM((n_prob, 2 * CHUNK, PAIR), BF16),
           pltpu.VMEM((n_prob, CHUNK, PAIR), F32),
           pltpu.VMEM((n_prob, CHUNK, PAIR), BF16),
           pltpu.VMEM((n_prob, CHUNK, PAIR), F32),
           pltpu.VMEM((n_prob, PAIR, CHUNK), BF16),
           pltpu.VMEM((n_prob, PAIR, PAIR), F32),
           pltpu.VMEM((n_prob, PAIR, PAIR), F32)],
        compiler_params=pltpu.CompilerParams(
            dimension_semantics=("parallel", "arbitrary"), vmem_limit_bytes=VMEM_LIMIT),
        name="rwkv",
    )(p, *smalls)


def _selection_constants(tb):
    i = jnp.arange(tb)
    lblk = ((i[:, None] // CHUNK == i[None, :] // CHUNK) & (i[:, None] >= i[None, :])).astype(BF16)
    c = jnp.arange(PAIR)
    e128 = (c[:, None] // HEAD_DIM == c[None, :] // HEAD_DIM).astype(BF16)
    return lblk, e128


def kernel(x, ffn1_pre_g, ffn1_w_gate, ffn1_w_up, ffn1_w_down, ffn1_post_g, mix_pre_g, w_in, gdn_conv_w, gdn_a_log, gdn_dt_bias, gdn_norm_g, rwkv_mu, rwkv_w0, rwkv_w2, rwkv_a0, rwkv_a2, rwkv_g2, rwkv_k_k, rwkv_k_a, rwkv_r_k, rwkv_ln_g, rwkv_ln_b, w_out, mix_post_g, ffn2_pre_g, ffn2_w_gate, ffn2_w_up, ffn2_w_down, ffn2_post_g):
    b, t, d = x.shape
    depth = ffn1_pre_g.shape[0]
    consts = _selection_constants(CHUNK)
    rwkv_consts = _selection_constants(RWKV_PREP_ROWS)
    row = lambda v: v.reshape(1, -1).astype(F32)
    h = x.reshape(b * t, d)
    for l in range(depth):
        h = _ffn(h, row(ffn1_pre_g[l]), ffn1_w_gate[l].astype(BF16), ffn1_w_up[l].astype(BF16),
                 ffn1_w_down[l].astype(BF16), row(ffn1_post_g[l]))

        wl = w_in[l]
        n_qkvz = 4 * D_MIX
        w_cat = jnp.concatenate(
            [wl[:, :n_qkvz], wl[:, n_qkvz:n_qkvz + 2 * N_HEADS],
             jnp.zeros((d, GDN_SMALL - 2 * N_HEADS), wl.dtype), wl[:, n_qkvz + 2 * N_HEADS:]], axis=1)
        x_main, x_small, x_rwkv = _in_proj(h, row(mix_pre_g[l]), w_cat.astype(BF16),
                                           gdn_conv_w[l].astype(F32), row(rwkv_mu[l]), t)

        pad_small = lambda v: jnp.zeros((1, GDN_SMALL), F32).at[0, N_HEADS:2 * N_HEADS].set(v)
        y_gdn = _gdn(x_main.reshape(b, t, GDN_MAIN), x_small.reshape(b, t, GDN_SMALL),
                     pad_small(gdn_a_log[l]), pad_small(gdn_dt_bias[l]),
                     row(jnp.tile(gdn_norm_g[l], N_HEADS)), consts)

        zeros_lora = jnp.zeros((LORA_W, D_MIX), F32)
        w2_pad = jnp.concatenate([rwkv_w2[l], zeros_lora], axis=0).astype(BF16)
        a2_pad = jnp.concatenate([zeros_lora, rwkv_a2[l]], axis=0).astype(BF16)
        rwkv_params = (row(rwkv_w0[l]), row(rwkv_a0[l]), w2_pad, a2_pad,
                       rwkv_g2[l].astype(BF16), row(rwkv_k_k[l]), row(rwkv_k_a[l]), row(rwkv_r_k[l]),
                       row(rwkv_ln_g[l]), row(rwkv_ln_b[l]))
        y_rwkv = _rwkv(x_rwkv.reshape(b, t, RWKV_COLS), rwkv_params, rwkv_consts)

        h = _ffn(h, row(ffn2_pre_g[l]), ffn2_w_gate[l].astype(BF16), ffn2_w_up[l].astype(BF16),
                 ffn2_w_down[l].astype(BF16), row(ffn2_post_g[l]),
                 mix=(y_gdn.reshape(b * t, D_MIX), y_rwkv.reshape(b * t, D_MIX),
                      w_out[l].astype(BF16), row(mix_post_g[l])))
    return h.reshape(b, t, d)
```

```python
import functools

import jax
import jax.numpy as jnp
from jax import lax
from jax.experimental import pallas as pl
from jax.experimental.pallas import tpu as pltpu

F32 = jnp.float32
BF16 = jnp.bfloat16

D_MODEL = 1024
D_FF = 2816
HEAD_DIM = 64
N_HEADS = 8
D_MIX = N_HEADS * HEAD_DIM
N_PAIRS = N_HEADS // 2
PAIR = 2 * HEAD_DIM
CHUNK = 64
CONV_WIDTH = 4
LORA_W = 64
LORA_A = 64
LORA_G = 128
EPS = 1e-6
L2_EPS = 1e-6
GN_EPS = 64e-5

LANES = 128
SUBLANES = 8
GDN_MAIN = 4 * D_MIX
GDN_SMALL = LANES
RWKV_COLS = 3 * D_MIX + LORA_W + LORA_A + LORA_G

FFN_TM = 1024
FFN_PART = 256
PROJ_TM = 512
MIX_TB = 512
RWKV_PREP_ROWS = 256
VMEM_LIMIT = 56 * 1024 * 1024


def _dot(a, b):
    return jnp.dot(a.astype(BF16), b.astype(BF16), preferred_element_type=F32)


def _dot_nt(a, b):
    return lax.dot_general(a.astype(BF16), b.astype(BF16), (((1,), (1,)), ((), ())),
                           preferred_element_type=F32)


def _dot_tn(a, b):
    return lax.dot_general(a.astype(BF16), b.astype(BF16), (((0,), (0,)), ((), ())),
                           preferred_element_type=F32)


def _split(x, passes):
    pieces = []
    rem = x
    for i in range(passes):
        p = rem.astype(BF16)
        pieces.append(p)
        if i + 1 < passes:
            rem = rem - p.astype(F32)
    return pieces


def _head_sums(x, e128):
    xb = x.astype(BF16)
    return jnp.concatenate(
        [jnp.dot(xb[:, g * PAIR:(g + 1) * PAIR], e128, preferred_element_type=F32)
         for g in range(N_PAIRS)], axis=1)


def _sel_dot(sel, x, passes=3):
    out = None
    for p in _split(x, passes):
        t = jnp.dot(sel, p, preferred_element_type=F32)
        out = t if out is None else out + t
    return out


def _rmsnorm(x, g):
    return x * lax.rsqrt(jnp.mean(x * x, axis=-1, keepdims=True) + EPS) * g


def _sigmoid(x):
    return jax.nn.sigmoid(x)


def _silu(x):
    hx = 0.5 * x
    return hx + hx * jnp.tanh(hx)


def _softplus(x):
    return jnp.maximum(x, 0.0) + jnp.log(1.0 + jnp.exp(-jnp.abs(x)))


def _bd(x):
    x = x.astype(BF16)
    lane = lax.broadcasted_iota(jnp.int32, x.shape, 1)
    zero = jnp.zeros_like(x)
    return jnp.concatenate([jnp.where(lane < HEAD_DIM, x, zero),
                            jnp.where(lane >= HEAD_DIM, x, zero)], axis=0)


def _advance(filler):
    if filler is not None:
        next(filler, None)


def _drain(stages):
    for _ in stages:
        pass


def _skewed(parts):
    live = []
    pending = list(parts)
    while pending or live:
        if pending:
            live.append(pending.pop(0))
        for g in list(live):
            if next(g, StopIteration) is StopIteration:
                live.remove(g)


def _tri_inverse_many(xs, ti, tj, filler=None):
    def sub_blocks(b):
        return ((ti ^ tj) < 2 * b) & ((ti & b) != 0) & ((tj & b) == 0)

    eye = jnp.where(ti == tj, 1.0, 0.0)
    ts = [(eye - jnp.where(sub_blocks(1), x, 0.0)).astype(BF16) for x in xs]
    xbs = [x.astype(BF16) for x in xs]
    zero = jnp.zeros((CHUNK, PAIR), BF16)
    b = 2
    while b < CHUNK:
        m = sub_blocks(b)
        tls = [jnp.dot(t, _bd(jnp.where(m, xb, zero)), preferred_element_type=F32)
               for t, xb in zip(ts, xbs)]
        _advance(filler)
        ts = [jnp.dot((eye - tl).astype(BF16), _bd(t), preferred_element_type=F32).astype(BF16)
              for tl, t in zip(tls, ts)]
        _advance(filler)
        b *= 2
    return ts


def _pair_masks():
    ti = lax.broadcasted_iota(jnp.int32, (CHUNK, PAIR), 0)
    tj = lax.broadcasted_iota(jnp.int32, (CHUNK, PAIR), 1) & (HEAD_DIM - 1)
    ii = lax.broadcasted_iota(jnp.int32, (PAIR, PAIR), 0)
    jj = lax.broadcasted_iota(jnp.int32, (PAIR, PAIR), 1)
    return ti, tj, ti >= tj, ti > tj, (ii ^ jj) < HEAD_DIM


def _ffn_body(with_mix, *refs):
    if with_mix:
        (yg_ref, yr_ref, h_ref, wout_ref, mixg_ref, preg_ref, wg_ref, wu_ref, wd_ref, postg_ref,
         o_ref) = refs
    else:
        (h_ref, preg_ref, wg_ref, wu_ref, wd_ref, postg_ref, o_ref) = refs

    def half(rows):
        h = h_ref[rows, :]
        if with_mix:
            y = jnp.concatenate([yg_ref[rows, :], yr_ref[rows, :]], axis=-1)
            h = h + _rmsnorm(_dot(y, wout_ref[...]), mixg_ref[...])
        xn = _rmsnorm(h, preg_ref[...]).astype(BF16)
        yield
        gate = jnp.dot(xn, wg_ref[...], preferred_element_type=F32)
        up = jnp.dot(xn, wu_ref[...], preferred_element_type=F32)
        yield
        hid = (_silu(gate) * up).astype(BF16)
        f = jnp.dot(hid, wd_ref[...], preferred_element_type=F32)
        yield
        o_ref[rows, :] = h + 0.5 * _rmsnorm(f, postg_ref[...])
        yield

    _skewed([half(pl.ds(i * FFN_PART, FFN_PART)) for i in range(FFN_TM // FFN_PART)])


def _ffn(h, pre_g, w_gate, w_up, w_down, post_g, mix=None):
    m = h.shape[0]
    row = lambda width: pl.BlockSpec((FFN_TM, width), lambda i: (i, 0))
    resident = pl.BlockSpec(memory_space=pltpu.VMEM)
    in_specs, args = [], []
    if mix is not None:
        y_gdn, y_rwkv, w_out, mix_g = mix
        in_specs += [row(D_MIX), row(D_MIX)]
        args += [y_gdn, y_rwkv]
    in_specs.append(row(D_MODEL))
    args.append(h)
    if mix is not None:
        in_specs += [resident, resident]
        args += [w_out, mix_g]
    in_specs += [resident] * 5
    args += [pre_g, w_gate, w_up, w_down, post_g]
    return pl.pallas_call(
        functools.partial(_ffn_body, mix is not None),
        grid=(m // FFN_TM,),
        in_specs=in_specs,
        out_specs=row(D_MODEL),
        out_shape=jax.ShapeDtypeStruct((m, D_MODEL), F32),
        compiler_params=pltpu.CompilerParams(
            dimension_semantics=("parallel",), vmem_limit_bytes=VMEM_LIMIT),
        name="ffn_mix" if mix is not None else "ffn",
    )(*args)


def _proj_body(tiles_per_seq, h_ref, g_ref, w_ref, convw_ref, mu_ref, main_ref, small_ref, rwkv_ref,
               qkv_buf, rwkv_buf):
    tm = PROJ_TM

    @pl.when(pl.program_id(0) % tiles_per_seq == 0)
    def _():
        qkv_buf[0:SUBLANES, :] = jnp.zeros((SUBLANES, 3 * D_MIX), F32)
        rwkv_buf[0:SUBLANES, :] = jnp.zeros((SUBLANES, RWKV_COLS), F32)

    def half(r0, n):
        rows = pl.ds(r0, n)
        stored = pl.ds(SUBLANES + r0, n)
        window = pl.ds(r0, SUBLANES + n)
        xn = _rmsnorm(h_ref[rows, :], g_ref[...]).astype(BF16)
        proj = lambda lo, hi: jnp.dot(xn, w_ref[:, lo:hi], preferred_element_type=F32)
        yield
        rwkv_buf[stored, :] = proj(GDN_MAIN + GDN_SMALL, GDN_MAIN + GDN_SMALL + RWKV_COLS)
        qkv_buf[stored, :] = proj(0, 3 * D_MIX)
        yield
        pall = rwkv_buf[window, :]
        cur = pall[SUBLANES:]
        prev = pltpu.roll(pall, 1, axis=0)[SUBLANES:]
        rwkv_ref[rows, :] = cur + (prev - cur) * mu_ref[...]
        xall = qkv_buf[window, :]
        conv = xall[SUBLANES:] * convw_ref[CONV_WIDTH - 1:CONV_WIDTH, :]
        for d in range(1, CONV_WIDTH):
            shifted = pltpu.roll(xall, d, axis=0)[SUBLANES:]
            conv = conv + shifted * convw_ref[CONV_WIDTH - 1 - d:CONV_WIDTH - d, :]
        main_ref[rows, 0:3 * D_MIX] = _silu(conv)
        main_ref[rows, 3 * D_MIX:] = proj(3 * D_MIX, GDN_MAIN)
        small_ref[rows, :] = proj(GDN_MAIN, GDN_MAIN + GDN_SMALL)
        yield

    _skewed([half(0, tm // 2), half(tm // 2, tm // 2)])

    qkv_buf[0:SUBLANES, :] = qkv_buf[tm:tm + SUBLANES, :]
    rwkv_buf[0:SUBLANES, :] = rwkv_buf[tm:tm + SUBLANES, :]


def _in_proj(h, g, w, conv_w, mu, seq_len):
    m = h.shape[0]
    n = w.shape[1]
    row = lambda width: pl.BlockSpec((PROJ_TM, width), lambda i: (i, 0))
    const = lambda arr: pl.BlockSpec(arr.shape, lambda i: (0, 0))
    return pl.pallas_call(
        functools.partial(_proj_body, seq_len // PROJ_TM),
        grid=(m // PROJ_TM,),
        in_specs=[row(D_MODEL), const(g), pl.BlockSpec((D_MODEL, n), lambda i: (0, 0)),
                  const(conv_w), const(mu)],
        out_specs=[row(GDN_MAIN), row(GDN_SMALL), row(RWKV_COLS)],
        out_shape=[jax.ShapeDtypeStruct((m, GDN_MAIN), F32),
                   jax.ShapeDtypeStruct((m, GDN_SMALL), F32),
                   jax.ShapeDtypeStruct((m, RWKV_COLS), F32)],
        scratch_shapes=[pltpu.VMEM((PROJ_TM + SUBLANES, 3 * D_MIX), F32),
                        pltpu.VMEM((PROJ_TM + SUBLANES, RWKV_COLS), F32)],
        compiler_params=pltpu.CompilerParams(
            dimension_semantics=("arbitrary",), vmem_limit_bytes=VMEM_LIMIT),
        name="in_proj",
    )(h, g, w, conv_w, mu)


def _gdn_precompute(loads, masks, wq_s, u_s, attn_s, kdt_s, base=0, filler=None):
    ti, tj, causal, strict, _ = masks
    a_list, rest = [], []
    for q, k, v, beta, g_nat in loads:
        g_last = g_nat[CHUNK - 1:CHUNK, :]
        e_g = jnp.exp(g_nat)
        kb = k * beta
        g_row = jnp.sum(jnp.where(ti == tj, g_nat, 0.0), axis=0, keepdims=True)
        decay = jnp.exp(jnp.where(causal, g_nat - g_row, -jnp.inf))
        gram = _dot_nt(jnp.concatenate([kb, q], axis=0), _bd(k))
        a_list.append(jnp.where(strict, gram[:CHUNK] * decay, 0.0))
        attn = (gram[CHUNK:] * decay).astype(BF16)
        rhs = jnp.concatenate([_bd(v * beta), _bd(kb * e_g)], axis=1)
        kdt = (k * jnp.exp(g_last - g_nat)).T.astype(BF16)
        rest.append((attn, rhs, (q * e_g).astype(BF16), kdt))
    ts = _tri_inverse_many(a_list, ti, tj, filler)
    for n, (t, (attn, rhs, qg, kdt)) in enumerate(zip(ts, rest), start=base):
        uw = jnp.dot(t, rhs, preferred_element_type=F32)
        wq_s[n] = jnp.concatenate([uw[:, PAIR:].astype(BF16), qg], axis=0)
        u_s[n] = uw[:, :PAIR]
        attn_s[n] = attn
        kdt_s[n] = kdt


def _gdn_body(xm_ref, xs_ref, alog_ref, dtb_ref, ng_ref, lblk_ref, e128_ref,
              o_ref, q_s, k_s, beta_s, g_s, o_s, state,
              wq_s, u_s, attn_s, kdt_s):
    tb = MIX_TB

    @pl.when(pl.program_id(1) == 0)
    def _():
        state[...] = jnp.zeros_like(state)

    e128 = e128_ref[...]
    masks = _pair_masks()
    same_head = masks[-1]
    n_chunks = tb // CHUNK
    first_head = lax.broadcasted_iota(jnp.int32, (CHUNK, LANES), 1) < HEAD_DIM
    lane_bcast = lambda x, j: jnp.broadcast_to(x[:, j:j + 1], (CHUNK, LANES))
    g_last = {}

    def prepare(chunks):
        for c in chunks:
            rows = pl.ds(c * CHUNK, CHUNK)
            q = xm_ref[0, rows, 0:D_MIX]
            k = xm_ref[0, rows, D_MIX:2 * D_MIX]
            q_s[rows, :] = q * lax.rsqrt(_head_sums(q * q, e128) + L2_EPS) * (HEAD_DIM ** -0.5)
            k_s[rows, :] = k * lax.rsqrt(_head_sums(k * k, e128) + L2_EPS)
            xs = xs_ref[0, rows, :]
            beta = _sigmoid(xs)
            g_small = -jnp.exp(alog_ref[...]) * _softplus(xs + dtb_ref[...])
            g_cum = _sel_dot(lblk_ref[...], g_small)
            g_last[c] = g_cum[CHUNK - 1:CHUNK, :]
            for p in range(N_PAIRS):
                cols = pl.ds(p * PAIR, PAIR)
                g_s[rows, cols] = jnp.where(first_head, lane_bcast(g_cum, N_HEADS + 2 * p),
                                            lane_bcast(g_cum, N_HEADS + 2 * p + 1))
                beta_s[rows, cols] = jnp.where(first_head, lane_bcast(beta, 2 * p),
                                               lane_bcast(beta, 2 * p + 1))
            yield

    def finish(chunks):
        for c in chunks:
            rows = pl.ds(c * CHUNK, CHUNK)
            o = o_s[rows, :]
            z = xm_ref[0, rows, 3 * D_MIX:4 * D_MIX]
            ms = _head_sums(o * o, e128) * (1.0 / HEAD_DIM)
            o_ref[0, rows, :] = o * lax.rsqrt(ms + EPS) * ng_ref[...] * _silu(z)
            yield

    def loads_of(chunks):
        loads = []
        for c in chunks:
            rows = pl.ds(c * CHUNK, CHUNK)
            for p in range(N_PAIRS):
                cols = pl.ds(p * PAIR, PAIR)
                v = xm_ref[0, rows, pl.ds(2 * D_MIX + p * PAIR, PAIR)]
                loads.append((q_s[rows, cols], k_s[rows, cols], v, beta_s[rows, cols], g_s[rows, cols]))
        return loads

    s = [state[p] for p in range(N_PAIRS)]

    def sequential(chunks):
        for c in chunks:
            rows = pl.ds(c * CHUNK, CHUNK)
            ws_qs = [jnp.dot(wq_s[c * N_PAIRS + p], s[p].astype(BF16), preferred_element_type=F32)
                     for p in range(N_PAIRS)]
            yield
            v_new = [(u_s[c * N_PAIRS + p] - ws_qs[p][:CHUNK]).astype(BF16) for p in range(N_PAIRS)]
            e_last = jnp.exp(g_last[c])
            for p in range(N_PAIRS):
                head_decay = lambda h: jnp.broadcast_to(
                    e_last[:, N_HEADS + h:N_HEADS + h + 1], (HEAD_DIM, PAIR))
                s_decay = jnp.concatenate([head_decay(2 * p), head_decay(2 * p + 1)], axis=0)
                kv = jnp.dot(kdt_s[c * N_PAIRS + p], v_new[p], preferred_element_type=F32)
                s[p] = s[p] * s_decay + jnp.where(same_head, kv, 0.0)
            yield
            for p in range(N_PAIRS):
                o_s[rows, pl.ds(p * PAIR, PAIR)] = ws_qs[p][CHUNK:] + jnp.dot(
                    attn_s[c * N_PAIRS + p], _bd(v_new[p]), preferred_element_type=F32)
            yield

    half = n_chunks // 2 + 1
    first, second = range(half), range(half, n_chunks)
    scratch = (wq_s, u_s, attn_s, kdt_s)
    _drain(prepare(first))
    filler = prepare(second)
    _gdn_precompute(loads_of(first), masks, *scratch, filler=filler)
    _drain(filler)
    filler = sequential(first)
    _gdn_precompute(loads_of(second), masks, *scratch, base=half * N_PAIRS, filler=filler)
    _drain(filler)
    filler = finish(first)
    for _ in sequential(second):
        _advance(filler)
    _drain(filler)
    _drain(finish(second))
    for p in range(N_PAIRS):
        state[p] = s[p]


def _gdn(x_main, x_small, a_log_pad, dt_bias_pad, norm_g_full, consts):
    b, t, _ = x_main.shape
    tb = MIX_TB
    lblk, e128 = consts
    n_prob = (tb // CHUNK) * N_PAIRS
    const = lambda arr: pl.BlockSpec(arr.shape, lambda i, j: (0,) * arr.ndim)
    smalls = [a_log_pad, dt_bias_pad, norm_g_full, lblk, e128]
    return pl.pallas_call(
        _gdn_body,
        grid=(b, t // tb),
        in_specs=[pl.BlockSpec((1, tb, GDN_MAIN), lambda i, j: (i, j, 0)),
                  pl.BlockSpec((1, tb, GDN_SMALL), lambda i, j: (i, j, 0))] + [const(a) for a in smalls],
        out_specs=pl.BlockSpec((1, tb, D_MIX), lambda i, j: (i, j, 0)),
        out_shape=jax.ShapeDtypeStruct((b, t, D_MIX), F32),
        scratch_shapes=[pltpu.VMEM((tb, D_MIX), F32)] * 5
        + [pltpu.VMEM((N_PAIRS, PAIR, PAIR), F32),
           pltpu.VMEM((n_prob, 2 * CHUNK, PAIR), BF16),
           pltpu.VMEM((n_prob, CHUNK, PAIR), F32),
           pltpu.VMEM((n_prob, CHUNK, PAIR), BF16),
           pltpu.VMEM((n_prob, PAIR, CHUNK), BF16)],
        compiler_params=pltpu.CompilerParams(
            dimension_semantics=("parallel", "arbitrary"), vmem_limit_bytes=VMEM_LIMIT),
        name="gdn",
    )(x_main, x_small, *smalls)


def _rwkv_precompute(loads, masks, wr_s, ut_s, arb_s, y0_s, btt_s, z0_s, pc_s, base=0, filler=None):
    ti, tj, causal, strict, same_head = masks
    x_list, rest = [], []
    for r, k, v, aa, bb, lw, lp in loads:
        p_inv = jnp.exp(-lp)
        lp_last = lp[CHUNK - 1:CHUNK, :]
        p_rest = jnp.exp(lp_last - lp)
        a_h = (aa * jnp.exp(lp - lw)).astype(BF16)
        r_h = (r * jnp.exp(lp)).astype(BF16)
        gram = _dot_nt(jnp.concatenate([a_h, r_h], axis=0),
                       jnp.concatenate([_bd(bb * p_inv), _bd(k * p_inv)], axis=0))
        x_list.append(jnp.where(strict, -gram[:CHUNK, :PAIR], 0.0))
        a_ak = jnp.where(strict, gram[:CHUNK, PAIR:], 0.0)
        a_rb = jnp.where(causal, gram[CHUNK:, :PAIR], 0.0).astype(BF16)
        a_rk = jnp.where(causal, gram[CHUNK:, PAIR:], 0.0)
        av = _dot(jnp.concatenate([a_ak, a_rk], axis=0), _bd(v))
        btt = (bb * p_rest).T.astype(BF16)
        z0 = jnp.where(same_head, _dot_tn(k * p_rest, v), 0.0)
        pc = jnp.broadcast_to(jnp.exp(lp_last), (PAIR, PAIR)).T
        rest.append((a_h, r_h, a_rb, av, btt, z0, pc))
    ts = _tri_inverse_many(x_list, ti, tj, filler)
    for n, (t, (a_h, r_h, a_rb, av, btt, z0, pc)) in enumerate(zip(ts, rest), start=base):
        wu = jnp.dot(t, jnp.concatenate([_bd(a_h), _bd(av[:CHUNK])], axis=1),
                     preferred_element_type=F32)
        wr_s[n] = jnp.concatenate([wu[:, :PAIR].astype(BF16), r_h], axis=0)
        ut_s[n] = wu[:, PAIR:]
        arb_s[n] = a_rb
        y0_s[n] = av[CHUNK:]
        btt_s[n] = btt
        z0_s[n] = z0
        pc_s[n] = pc


def _rwkv_body(p_ref, w0_ref, a0_ref, w2_ref, a2_ref, g2_ref, kk_ref, ka_ref, rk_ref,
               lng_ref, lnb_ref, lblk_ref, e128_ref, o_ref,
               k_s, aa_s, bb_s, lw_s, lp_s, gate_s, y_s, state,
               wr_s, ut_s, arb_s, y0_s, btt_s, z0_s, pc_s):
    tb = MIX_TB

    @pl.when(pl.program_id(1) == 0)
    def _():
        state[...] = jnp.zeros_like(state)

    v_cols = 2 * D_MIX
    e128 = e128_ref[...]
    masks = _pair_masks()
    same_head = masks[-1]
    n_chunks = tb // CHUNK

    def prepare(chunks):
        slab = lblk_ref.shape[0]
        for r0 in range(chunks[0] * CHUNK, (chunks[-1] + 1) * CHUNK, slab):
            rows = pl.ds(r0, slab)
            k = p_ref[0, rows, D_MIX:2 * D_MIX]
            lora_wa = p_ref[0, rows, 3 * D_MIX:3 * D_MIX + LORA_W + LORA_A]
            g_lo = p_ref[0, rows, 3 * D_MIX + LORA_W + LORA_A:]
            w = -_softplus(-(w0_ref[...] + _dot(jnp.tanh(lora_wa), w2_ref[...]))) - 0.5
            lw = -jnp.exp(w)
            a = _sigmoid(a0_ref[...] + _dot(lora_wa, a2_ref[...]))
            gate_s[rows, :] = _dot(_sigmoid(g_lo), g2_ref[...])
            kk = k * kk_ref[...]
            kk = kk * lax.rsqrt(_head_sums(kk * kk, e128) + L2_EPS)
            k_s[rows, :] = k * (1.0 + (a - 1.0) * ka_ref[...])
            aa_s[rows, :] = -kk
            bb_s[rows, :] = kk * a
            lw_s[rows, :] = lw
            lp_s[rows, :] = _sel_dot(lblk_ref[...], lw)
            yield

    def finish(chunks):
        inv_d = 1.0 / HEAD_DIM
        for c in chunks:
            rows = pl.ds(c * CHUNK, CHUNK)
            y = y_s[rows, :]
            yc = y - _head_sums(y, e128) * inv_d
            y = yc * lax.rsqrt(_head_sums(yc * yc, e128) * inv_d + GN_EPS) * lng_ref[...] + lnb_ref[...]
            r = p_ref[0, rows, 0:D_MIX]
            bonus = _head_sums(r * k_s[rows, :] * rk_ref[...], e128) * p_ref[0, rows, v_cols:v_cols + D_MIX]
            o_ref[0, rows, :] = (y + bonus) * gate_s[rows, :]
            yield

    def loads_of(chunks):
        loads = []
        for c in chunks:
            rows = pl.ds(c * CHUNK, CHUNK)
            for pr in range(N_PAIRS):
                cols = pl.ds(pr * PAIR, PAIR)
                r = p_ref[0, rows, cols]
                v = p_ref[0, rows, pl.ds(v_cols + pr * PAIR, PAIR)]
                loads.append((r, k_s[rows, cols], v, aa_s[rows, cols],
                              bb_s[rows, cols], lw_s[rows, cols], lp_s[rows, cols]))
        return loads

    hs = [state[pr] for pr in range(N_PAIRS)]

    def sequential(chunks):
        for c in chunks:
            rows = pl.ds(c * CHUNK, CHUNK)
            wh_rh = [jnp.dot(wr_s[c * N_PAIRS + pr], hs[pr].astype(BF16), preferred_element_type=F32)
                     for pr in range(N_PAIRS)]
            yield
            u = [(wh_rh[pr][:CHUNK] + ut_s[c * N_PAIRS + pr]).astype(BF16) for pr in range(N_PAIRS)]
            for pr in range(N_PAIRS):
                n = c * N_PAIRS + pr
                bu = jnp.dot(btt_s[n], u[pr], preferred_element_type=F32)
                hs[pr] = pc_s[n] * hs[pr] + jnp.where(same_head, bu, 0.0) + z0_s[n]
            yield
            for pr in range(N_PAIRS):
                n = c * N_PAIRS + pr
                y_s[rows, pl.ds(pr * PAIR, PAIR)] = (
                    wh_rh[pr][CHUNK:] + jnp.dot(arb_s[n], _bd(u[pr]), preferred_element_type=F32)
                    + y0_s[n])
            yield

    half = n_chunks // 2
    first, second = range(half), range(half, n_chunks)
    scratch = (wr_s, ut_s, arb_s, y0_s, btt_s, z0_s, pc_s)
    _drain(prepare(first))
    filler = prepare(second)
    _rwkv_precompute(loads_of(first), masks, *scratch, filler=filler)
    _drain(filler)
    filler = sequential(first)
    _rwkv_precompute(loads_of(second), masks, *scratch, base=half * N_PAIRS, filler=filler)
    _drain(filler)
    filler = finish(first)
    for _ in sequential(second):
        _advance(filler)
    _drain(filler)
    _drain(finish(second))
    for pr in range(N_PAIRS):
        state[pr] = hs[pr]


def _rwkv(p, params, consts):
    b, t, _ = p.shape
    tb = MIX_TB
    lblk, e128 = consts
    n_prob = (tb // CHUNK) * N_PAIRS
    const = lambda arr: pl.BlockSpec(arr.shape, lambda i, j: (0,) * arr.ndim)
    smalls = list(params) + [lblk, e128]
    return pl.pallas_call(
        _rwkv_body,
        grid=(b, t // tb),
        in_specs=[pl.BlockSpec((1, tb, RWKV_COLS), lambda i, j: (i, j, 0))] + [const(a) for a in smalls],
        out_specs=pl.BlockSpec((1, tb, D_MIX), lambda i, j: (i, j, 0)),
        out_shape=jax.ShapeDtypeStruct((b, t, D_MIX), F32),
        scratch_shapes=[pltpu.VMEM((tb, D_MIX), F32)] * 7
        + [pltpu.VMEM((N_PAIRS, PAIR, PAIR), F32),
           pltpu.VMEM((n_prob, 2 * CHUNK, PAIR), BF16),
           pltpu.VMEM((n_prob, CHUNK, PAIR), F32),
           pltpu.VMEM((n_prob, CHUNK, PAIR), BF16),
           pltpu.VMEM((n_prob, CHUNK, PAIR), F32),
           pltpu.VMEM((n_prob, PAIR, CHUNK), BF16),
           pltpu.VMEM((n_prob, PAIR, PAIR), F32),
           pltpu.VMEM((n_prob, PAIR, PAIR), F32)],
        compiler_params=pltpu.CompilerParams(
            dimension_semantics=("parallel", "arbitrary"), vmem_limit_bytes=VMEM_LIMIT),
        name="rwkv",
    )(p, *smalls)


def _selection_constants(tb):
    i = jnp.arange(tb)
    lblk = ((i[:, None] // CHUNK == i[None, :] // CHUNK) & (i[:, None] >= i[None, :])).astype(BF16)
    c = jnp.arange(PAIR)
    e128 = (c[:, None] // HEAD_DIM == c[None, :] // HEAD_DIM).astype(BF16)
    return lblk, e128


def kernel(x, ffn1_pre_g, ffn1_w_gate, ffn1_w_up, ffn1_w_down, ffn1_post_g, mix_pre_g, w_in, gdn_conv_w, gdn_a_log, gdn_dt_bias, gdn_norm_g, rwkv_mu, rwkv_w0, rwkv_w2, rwkv_a0, rwkv_a2, rwkv_g2, rwkv_k_k, rwkv_k_a, rwkv_r_k, rwkv_ln_g, rwkv_ln_b, w_out, mix_post_g, ffn2_pre_g, ffn2_w_gate, ffn2_w_up, ffn2_w_down, ffn2_post_g):
    b, t, d = x.shape
    depth = ffn1_pre_g.shape[0]
    consts = _selection_constants(CHUNK)
    rwkv_consts = _selection_constants(RWKV_PREP_ROWS)
    row = lambda v: v.reshape(1, -1).astype(F32)
    h = x.reshape(b * t, d)
    for l in range(depth):
        h = _ffn(h, row(ffn1_pre_g[l]), ffn1_w_gate[l].astype(BF16), ffn1_w_up[l].astype(BF16),
                 ffn1_w_down[l].astype(BF16), row(ffn1_post_g[l]))

        wl = w_in[l]
        n_qkvz = 4 * D_MIX
        w_cat = jnp.concatenate(
            [wl[:, :n_qkvz], wl[:, n_qkvz:n_qkvz + 2 * N_HEADS],
             jnp.zeros((d, GDN_SMALL - 2 * N_HEADS), wl.dtype), wl[:, n_qkvz + 2 * N_HEADS:]], axis=1)
        x_main, x_small, x_rwkv = _in_proj(h, row(mix_pre_g[l]), w_cat.astype(BF16),
                                           gdn_conv_w[l].astype(F32), row(rwkv_mu[l]), t)

        pad_small = lambda v: jnp.zeros((1, GDN_SMALL), F32).at[0, N_HEADS:2 * N_HEADS].set(v)
        y_gdn = _gdn(x_main.reshape(b, t, GDN_MAIN), x_small.reshape(b, t, GDN_SMALL),
                     pad_small(gdn_a_log[l]), pad_small(gdn_dt_bias[l]),
                     row(jnp.tile(gdn_norm_g[l], N_HEADS)), consts)

        zeros_lora = jnp.zeros((LORA_W, D_MIX), F32)
        w2_pad = jnp.concatenate([rwkv_w2[l], zeros_lora], axis=0).astype(BF16)
        a2_pad = jnp.concatenate([zeros_lora, rwkv_a2[l]], axis=0).astype(BF16)
        rwkv_params = (row(rwkv_w0[l]), row(rwkv_a0[l]), w2_pad, a2_pad,
                       rwkv_g2[l].astype(BF16), row(rwkv_k_k[l]), row(rwkv_k_a[l]), row(rwkv_r_k[l]),
                       row(rwkv_ln_g[l]), row(rwkv_ln_b[l]))
        y_rwkv = _rwkv(x_rwkv.reshape(b, t, RWKV_COLS), rwkv_params, rwkv_consts)

        h = _ffn(h, row(ffn2_pre_g[l]), ffn2_w_gate[l].astype(BF16), ffn2_w_up[l].astype(BF16),
                 ffn2_w_down[l].astype(BF16), row(ffn2_post_g[l]),
                 mix=(y_gdn.reshape(b * t, D_MIX), y_rwkv.reshape(b * t, D_MIX),
                      w_out[l].astype(BF16), row(mix_post_g[l])))
    return h.reshape(b, t, d)
```

```python
import functools

import jax
import jax.numpy as jnp
from jax import lax
from jax.experimental import pallas as pl
from jax.experimental.pallas import tpu as pltpu

F32 = jnp.float32
BF16 = jnp.bfloat16

D_MODEL = 1024
D_FF = 2816
HEAD_DIM = 64
N_HEADS = 8
D_MIX = N_HEADS * HEAD_DIM
N_PAIRS = N_HEADS // 2
PAIR = 2 * HEAD_DIM
CHUNK = 64
CONV_WIDTH = 4
LORA_W = 64
LORA_A = 64
LORA_G = 128
EPS = 1e-6
L2_EPS = 1e-6
GN_EPS = 64e-5

LANES = 128
SUBLANES = 8
GDN_MAIN = 4 * D_MIX
GDN_SMALL = LANES
RWKV_COLS = 3 * D_MIX + LORA_W + LORA_A + LORA_G

FFN_TM = 1024
FFN_PART = 256
PROJ_TM = 512
MIX_TB = 512
RWKV_PREP_ROWS = 128
VMEM_LIMIT = 56 * 1024 * 1024


def _dot(a, b):
    return jnp.dot(a.astype(BF16), b.astype(BF16), preferred_element_type=F32)


def _dot_nt(a, b):
    return lax.dot_general(a.astype(BF16), b.astype(BF16), (((1,), (1,)), ((), ())),
                           preferred_element_type=F32)


def _dot_tn(a, b):
    return lax.dot_general(a.astype(BF16), b.astype(BF16), (((0,), (0,)), ((), ())),
                           preferred_element_type=F32)


def _split(x, passes):
    pieces = []
    rem = x
    for i in range(passes):
        p = rem.astype(BF16)
        pieces.append(p)
        if i + 1 < passes:
            rem = rem - p.astype(F32)
    return pieces


def _head_sums(x, e128):
    xb = x.astype(BF16)
    return jnp.concatenate(
        [jnp.dot(xb[:, g * PAIR:(g + 1) * PAIR], e128, preferred_element_type=F32)
         for g in range(N_PAIRS)], axis=1)


def _sel_dot(sel, x, passes=3):
    out = None
    for p in _split(x, passes):
        t = jnp.dot(sel, p, preferred_element_type=F32)
        out = t if out is None else out + t
    return out


def _rmsnorm(x, g):
    return x * lax.rsqrt(jnp.mean(x * x, axis=-1, keepdims=True) + EPS) * g


def _sigmoid(x):
    return jax.nn.sigmoid(x)


def _silu(x):
    hx = 0.5 * x
    return hx + hx * jnp.tanh(hx)


def _softplus(x):
    return jnp.maximum(x, 0.0) + jnp.log(1.0 + jnp.exp(-jnp.abs(x)))


def _bd(x):
    x = x.astype(BF16)
    lane = lax.broadcasted_iota(jnp.int32, x.shape, 1)
    zero = jnp.zeros_like(x)
    return jnp.concatenate([jnp.where(lane < HEAD_DIM, x, zero),
                            jnp.where(lane >= HEAD_DIM, x, zero)], axis=0)


def _advance(filler):
    if filler is not None:
        next(filler, None)


def _drain(stages):
    for _ in stages:
        pass


def _skewed(parts):
    live = []
    pending = list(parts)
    while pending or live:
        if pending:
            live.append(pending.pop(0))
        for g in list(live):
            if next(g, StopIteration) is StopIteration:
                live.remove(g)


def _tri_inverse_many(xs, ti, tj, filler=None):
    def sub_blocks(b):
        return ((ti ^ tj) < 2 * b) & ((ti & b) != 0) & ((tj & b) == 0)

    eye = jnp.where(ti == tj, 1.0, 0.0)
    ts = [(eye - jnp.where(sub_blocks(1), x, 0.0)).astype(BF16) for x in xs]
    xbs = [x.astype(BF16) for x in xs]
    zero = jnp.zeros((CHUNK, PAIR), BF16)
    b = 2
    while b < CHUNK:
        m = sub_blocks(b)
        tls = [jnp.dot(t, _bd(jnp.where(m, xb, zero)), preferred_element_type=F32)
               for t, xb in zip(ts, xbs)]
        _advance(filler)
        ts = [jnp.dot((eye - tl).astype(BF16), _bd(t), preferred_element_type=F32).astype(BF16)
              for tl, t in zip(tls, ts)]
        _advance(filler)
        b *= 2
    return ts


def _pair_masks():
    ti = lax.broadcasted_iota(jnp.int32, (CHUNK, PAIR), 0)
    tj = lax.broadcasted_iota(jnp.int32, (CHUNK, PAIR), 1) & (HEAD_DIM - 1)
    ii = lax.broadcasted_iota(jnp.int32, (PAIR, PAIR), 0)
    jj = lax.broadcasted_iota(jnp.int32, (PAIR, PAIR), 1)
    return ti, tj, ti >= tj, ti > tj, (ii ^ jj) < HEAD_DIM


def _ffn_body(with_mix, *refs):
    if with_mix:
        (yg_ref, yr_ref, h_ref, wout_ref, mixg_ref, preg_ref, wg_ref, wu_ref, wd_ref, postg_ref,
         o_ref) = refs
    else:
        (h_ref, preg_ref, wg_ref, wu_ref, wd_ref, postg_ref, o_ref) = refs

    def half(rows):
        h = h_ref[rows, :]
        if with_mix:
            y = jnp.concatenate([yg_ref[rows, :], yr_ref[rows, :]], axis=-1)
            h = h + _rmsnorm(_dot(y, wout_ref[...]), mixg_ref[...])
        xn = _rmsnorm(h, preg_ref[...]).astype(BF16)
        yield
        gate = jnp.dot(xn, wg_ref[...], preferred_element_type=F32)
        up = jnp.dot(xn, wu_ref[...], preferred_element_type=F32)
        yield
        hid = (_silu(gate) * up).astype(BF16)
        f = jnp.dot(hid, wd_ref[...], preferred_element_type=F32)
        yield
        o_ref[rows, :] = h + 0.5 * _rmsnorm(f, postg_ref[...])
        yield

    _skewed([half(pl.ds(i * FFN_PART, FFN_PART)) for i in range(FFN_TM // FFN_PART)])


def _ffn(h, pre_g, w_gate, w_up, w_down, post_g, mix=None):
    m = h.shape[0]
    row = lambda width: pl.BlockSpec((FFN_TM, width), lambda i: (i, 0))
    resident = pl.BlockSpec(memory_space=pltpu.VMEM)
    in_specs, args = [], []
    if mix is not None:
        y_gdn, y_rwkv, w_out, mix_g = mix
        in_specs += [row(D_MIX), row(D_MIX)]
        args += [y_gdn, y_rwkv]
    in_specs.append(row(D_MODEL))
    args.append(h)
    if mix is not None:
        in_specs += [resident, resident]
        args += [w_out, mix_g]
    in_specs += [resident] * 5
    args += [pre_g, w_gate, w_up, w_down, post_g]
    return pl.pallas_call(
        functools.partial(_ffn_body, mix is not None),
        grid=(m // FFN_TM,),
        in_specs=in_specs,
        out_specs=row(D_MODEL),
        out_shape=jax.ShapeDtypeStruct((m, D_MODEL), F32),
        compiler_params=pltpu.CompilerParams(
            dimension_semantics=("parallel",), vmem_limit_bytes=VMEM_LIMIT),
        name="ffn_mix" if mix is not None else "ffn",
    )(*args)


def _proj_body(tiles_per_seq, h_ref, g_ref, w_ref, convw_ref, mu_ref, main_ref, small_ref, rwkv_ref,
               qkv_buf, rwkv_buf):
    tm = PROJ_TM

    @pl.when(pl.program_id(0) % tiles_per_seq == 0)
    def _():
        qkv_buf[0:SUBLANES, :] = jnp.zeros((SUBLANES, 3 * D_MIX), F32)
        rwkv_buf[0:SUBLANES, :] = jnp.zeros((SUBLANES, RWKV_COLS), F32)

    def half(r0, n):
        rows = pl.ds(r0, n)
        stored = pl.ds(SUBLANES + r0, n)
        window = pl.ds(r0, SUBLANES + n)
        xn = _rmsnorm(h_ref[rows, :], g_ref[...]).astype(BF16)
        proj = lambda lo, hi: jnp.dot(xn, w_ref[:, lo:hi], preferred_element_type=F32)
        yield
        rwkv_buf[stored, :] = proj(GDN_MAIN + GDN_SMALL, GDN_MAIN + GDN_SMALL + RWKV_COLS)
        qkv_buf[stored, :] = proj(0, 3 * D_MIX)
        yield
        pall = rwkv_buf[window, :]
        cur = pall[SUBLANES:]
        prev = pltpu.roll(pall, 1, axis=0)[SUBLANES:]
        rwkv_ref[rows, :] = cur + (prev - cur) * mu_ref[...]
        xall = qkv_buf[window, :]
        conv = xall[SUBLANES:] * convw_ref[CONV_WIDTH - 1:CONV_WIDTH, :]
        for d in range(1, CONV_WIDTH):
            shifted = pltpu.roll(xall, d, axis=0)[SUBLANES:]
            conv = conv + shifted * convw_ref[CONV_WIDTH - 1 - d:CONV_WIDTH - d, :]
        main_ref[rows, 0:3 * D_MIX] = _silu(conv)
        main_ref[rows, 3 * D_MIX:] = proj(3 * D_MIX, GDN_MAIN)
        small_ref[rows, :] = proj(GDN_MAIN, GDN_MAIN + GDN_SMALL)
        yield

    _skewed([half(0, tm // 2), half(tm // 2, tm // 2)])

    qkv_buf[0:SUBLANES, :] = qkv_buf[tm:tm + SUBLANES, :]
    rwkv_buf[0:SUBLANES, :] = rwkv_buf[tm:tm + SUBLANES, :]


def _in_proj(h, g, w, conv_w, mu, seq_len):
    m = h.shape[0]
    n = w.shape[1]
    row = lambda width: pl.BlockSpec((PROJ_TM, width), lambda i: (i, 0))
    const = lambda arr: pl.BlockSpec(arr.shape, lambda i: (0, 0))
    return pl.pallas_call(
        functools.partial(_proj_body, seq_len // PROJ_TM),
        grid=(m // PROJ_TM,),
        in_specs=[row(D_MODEL), const(g), pl.BlockSpec((D_MODEL, n), lambda i: (0, 0)),
                  const(conv_w), const(mu)],
        out_specs=[row(GDN_MAIN), row(GDN_SMALL), row(RWKV_COLS)],
        out_shape=[jax.ShapeDtypeStruct((m, GDN_MAIN), F32),
                   jax.ShapeDtypeStruct((m, GDN_SMALL), F32),
                   jax.ShapeDtypeStruct((m, RWKV_COLS), F32)],
        scratch_shapes=[pltpu.VMEM((PROJ_TM + SUBLANES, 3 * D_MIX), F32),
                        pltpu.VMEM((PROJ_TM + SUBLANES, RWKV_COLS), F32)],
        compiler_params=pltpu.CompilerParams(
            dimension_semantics=("arbitrary",), vmem_limit_bytes=VMEM_LIMIT),
        name="in_proj",
    )(h, g, w, conv_w, mu)


def _gdn_precompute(loads, masks, wq_s, u_s, attn_s, kdt_s, base=0, filler=None):
    ti, tj, causal, strict, _ = masks
    a_list, rest = [], []
    for q, k, v, beta, g_nat in loads:
        g_last = g_nat[CHUNK - 1:CHUNK, :]
        e_g = jnp.exp(g_nat)
        kb = k * beta
        g_row = jnp.sum(jnp.where(ti == tj, g_nat, 0.0), axis=0, keepdims=True)
        decay = jnp.exp(jnp.where(causal, g_nat - g_row, -jnp.inf))
        gram = _dot_nt(jnp.concatenate([kb, q], axis=0), _bd(k))
        a_list.append(jnp.where(strict, gram[:CHUNK] * decay, 0.0))
        attn = (gram[CHUNK:] * decay).astype(BF16)
        rhs = jnp.concatenate([_bd(v * beta), _bd(kb * e_g)], axis=1)
        kdt = (k * jnp.exp(g_last - g_nat)).T.astype(BF16)
        rest.append((attn, rhs, (q * e_g).astype(BF16), kdt))
    ts = _tri_inverse_many(a_list, ti, tj, filler)
    for n, (t, (attn, rhs, qg, kdt)) in enumerate(zip(ts, rest), start=base):
        uw = jnp.dot(t, rhs, preferred_element_type=F32)
        wq_s[n] = jnp.concatenate([uw[:, PAIR:].astype(BF16), qg], axis=0)
        u_s[n] = uw[:, :PAIR]
        attn_s[n] = attn
        kdt_s[n] = kdt


def _gdn_body(xm_ref, xs_ref, alog_ref, dtb_ref, ng_ref, lblk_ref, e128_ref,
              o_ref, q_s, k_s, beta_s, g_s, o_s, state,
              wq_s, u_s, attn_s, kdt_s):
    tb = MIX_TB

    @pl.when(pl.program_id(1) == 0)
    def _():
        state[...] = jnp.zeros_like(state)

    e128 = e128_ref[...]
    masks = _pair_masks()
    same_head = masks[-1]
    n_chunks = tb // CHUNK
    first_head = lax.broadcasted_iota(jnp.int32, (CHUNK, LANES), 1) < HEAD_DIM
    lane_bcast = lambda x, j: jnp.broadcast_to(x[:, j:j + 1], (CHUNK, LANES))
    g_last = {}

    def prepare(chunks):
        for c in chunks:
            rows = pl.ds(c * CHUNK, CHUNK)
            q = xm_ref[0, rows, 0:D_MIX]
            k = xm_ref[0, rows, D_MIX:2 * D_MIX]
            q_s[rows, :] = q * lax.rsqrt(_head_sums(q * q, e128) + L2_EPS) * (HEAD_DIM ** -0.5)
            k_s[rows, :] = k * lax.rsqrt(_head_sums(k * k, e128) + L2_EPS)
            xs = xs_ref[0, rows, :]
            beta = _sigmoid(xs)
            g_small = -jnp.exp(alog_ref[...]) * _softplus(xs + dtb_ref[...])
            g_cum = _sel_dot(lblk_ref[...], g_small)
            g_last[c] = g_cum[CHUNK - 1:CHUNK, :]
            for p in range(N_PAIRS):
                cols = pl.ds(p * PAIR, PAIR)
                g_s[rows, cols] = jnp.where(first_head, lane_bcast(g_cum, N_HEADS + 2 * p),
                                            lane_bcast(g_cum, N_HEADS + 2 * p + 1))
                beta_s[rows, cols] = jnp.where(first_head, lane_bcast(beta, 2 * p),
                                               lane_bcast(beta, 2 * p + 1))
            yield

    def finish(chunks):
        for c in chunks:
            rows = pl.ds(c * CHUNK, CHUNK)
            o = o_s[rows, :]
            z = xm_ref[0, rows, 3 * D_MIX:4 * D_MIX]
            ms = _head_sums(o * o, e128) * (1.0 / HEAD_DIM)
            o_ref[0, rows, :] = o * lax.rsqrt(ms + EPS) * ng_ref[...] * _silu(z)
            yield

    def loads_of(chunks):
        loads = []
        for c in chunks:
            rows = pl.ds(c * CHUNK, CHUNK)
            for p in range(N_PAIRS):
                cols = pl.ds(p * PAIR, PAIR)
                v = xm_ref[0, rows, pl.ds(2 * D_MIX + p * PAIR, PAIR)]
                loads.append((q_s[rows, cols], k_s[rows, cols], v, beta_s[rows, cols], g_s[rows, cols]))
        return loads

    s = [state[p] for p in range(N_PAIRS)]

    def sequential(chunks):
        for c in chunks:
            rows = pl.ds(c * CHUNK, CHUNK)
            ws_qs = [jnp.dot(wq_s[c * N_PAIRS + p], s[p].astype(BF16), preferred_element_type=F32)
                     for p in range(N_PAIRS)]
            yield
            v_new = [(u_s[c * N_PAIRS + p] - ws_qs[p][:CHUNK]).astype(BF16) for p in range(N_PAIRS)]
            e_last = jnp.exp(g_last[c])
            for p in range(N_PAIRS):
                head_decay = lambda h: jnp.broadcast_to(
                    e_last[:, N_HEADS + h:N_HEADS + h + 1], (HEAD_DIM, PAIR))
                s_decay = jnp.concatenate([head_decay(2 * p), head_decay(2 * p + 1)], axis=0)
                kv = jnp.dot(kdt_s[c * N_PAIRS + p], v_new[p], preferred_element_type=F32)
                s[p] = s[p] * s_decay + jnp.where(same_head, kv, 0.0)
            yield
            for p in range(N_PAIRS):
                o_s[rows, pl.ds(p * PAIR, PAIR)] = ws_qs[p][CHUNK:] + jnp.dot(
                    attn_s[c * N_PAIRS + p], _bd(v_new[p]), preferred_element_type=F32)
            yield

    half = n_chunks // 2 + 1
    first, second = range(half), range(half, n_chunks)
    scratch = (wq_s, u_s, attn_s, kdt_s)
    _drain(prepare(first))
    filler = prepare(second)
    _gdn_precompute(loads_of(first), masks, *scratch, filler=filler)
    _drain(filler)
    filler = sequential(first)
    _gdn_precompute(loads_of(second), masks, *scratch, base=half * N_PAIRS, filler=filler)
    _drain(filler)
    filler = finish(first)
    for _ in sequential(second):
        _advance(filler)
    _drain(filler)
    _drain(finish(second))
    for p in range(N_PAIRS):
        state[p] = s[p]


def _gdn(x_main, x_small, a_log_pad, dt_bias_pad, norm_g_full, consts):
    b, t, _ = x_main.shape
    tb = MIX_TB
    lblk, e128 = consts
    n_prob = (tb // CHUNK) * N_PAIRS
    const = lambda arr: pl.BlockSpec(arr.shape, lambda i, j: (0,) * arr.ndim)
    smalls = [a_log_pad, dt_bias_pad, norm_g_full, lblk, e128]
    return pl.pallas_call(
        _gdn_body,
        grid=(b, t // tb),
        in_specs=[pl.BlockSpec((1, tb, GDN_MAIN), lambda i, j: (i, j, 0)),
                  pl.BlockSpec((1, tb, GDN_SMALL), lambda i, j: (i, j, 0))] + [const(a) for a in smalls],
        out_specs=pl.BlockSpec((1, tb, D_MIX), lambda i, j: (i, j, 0)),
        out_shape=jax.ShapeDtypeStruct((b, t, D_MIX), F32),
        scratch_shapes=[pltpu.VMEM((tb, D_MIX), F32)] * 5
        + [pltpu.VMEM((N_PAIRS, PAIR, PAIR), F32),
           pltpu.VMEM((n_prob, 2 * CHUNK, PAIR), BF16),
           pltpu.VMEM((n_prob, CHUNK, PAIR), F32),
           pltpu.VMEM((n_prob, CHUNK, PAIR), BF16),
           pltpu.VMEM((n_prob, PAIR, CHUNK), BF16)],
        compiler_params=pltpu.CompilerParams(
            dimension_semantics=("parallel", "arbitrary"), vmem_limit_bytes=VMEM_LIMIT),
        name="gdn",
    )(x_main, x_small, *smalls)


def _rwkv_precompute(loads, masks, wr_s, ut_s, arb_s, y0_s, btt_s, z0_s, pc_s, base=0, filler=None):
    ti, tj, causal, strict, same_head = masks
    x_list, rest = [], []
    for r, k, v, aa, bb, lw, lp in loads:
        p_inv = jnp.exp(-lp)
        lp_last = lp[CHUNK - 1:CHUNK, :]
        p_rest = jnp.exp(lp_last - lp)
        a_h = (aa * jnp.exp(lp - lw)).astype(BF16)
        r_h = (r * jnp.exp(lp)).astype(BF16)
        gram = _dot_nt(jnp.concatenate([a_h, r_h], axis=0),
                       jnp.concatenate([_bd(bb * p_inv), _bd(k * p_inv)], axis=0))
        x_list.append(jnp.where(strict, -gram[:CHUNK, :PAIR], 0.0))
        a_ak = jnp.where(strict, gram[:CHUNK, PAIR:], 0.0)
        a_rb = jnp.where(causal, gram[CHUNK:, :PAIR], 0.0).astype(BF16)
        a_rk = jnp.where(causal, gram[CHUNK:, PAIR:], 0.0)
        av = _dot(jnp.concatenate([a_ak, a_rk], axis=0), _bd(v))
        btt = (bb * p_rest).T.astype(BF16)
        z0 = jnp.where(same_head, _dot_tn(k * p_rest, v), 0.0)
        pc = jnp.broadcast_to(jnp.exp(lp_last), (PAIR, PAIR)).T
        rest.append((a_h, r_h, a_rb, av, btt, z0, pc))
    ts = _tri_inverse_many(x_list, ti, tj, filler)
    for n, (t, (a_h, r_h, a_rb, av, btt, z0, pc)) in enumerate(zip(ts, rest), start=base):
        wu = jnp.dot(t, jnp.concatenate([_bd(a_h), _bd(av[:CHUNK])], axis=1),
                     preferred_element_type=F32)
        wr_s[n] = jnp.concatenate([wu[:, :PAIR].astype(BF16), r_h], axis=0)
        ut_s[n] = wu[:, PAIR:]
        arb_s[n] = a_rb
        y0_s[n] = av[CHUNK:]
        btt_s[n] = btt
        z0_s[n] = z0
        pc_s[n] = pc


def _rwkv_body(p_ref, w0_ref, a0_ref, w2_ref, a2_ref, g2_ref, kk_ref, ka_ref, rk_ref,
               lng_ref, lnb_ref, lblk_ref, e128_ref, o_ref,
               k_s, aa_s, bb_s, lw_s, lp_s, gate_s, y_s, state,
               wr_s, ut_s, arb_s, y0_s, btt_s, z0_s, pc_s):
    tb = MIX_TB

    @pl.when(pl.program_id(1) == 0)
    def _():
        state[...] = jnp.zeros_like(state)

    v_cols = 2 * D_MIX
    e128 = e128_ref[...]
    masks = _pair_masks()
    same_head = masks[-1]
    n_chunks = tb // CHUNK

    def prepare(chunks):
        slab = lblk_ref.shape[0]
        for r0 in range(chunks[0] * CHUNK, (chunks[-1] + 1) * CHUNK, slab):
            rows = pl.ds(r0, slab)
            k = p_ref[0, rows, D_MIX:2 * D_MIX]
            lora_wa = p_ref[0, rows, 3 * D_MIX:3 * D_MIX + LORA_W + LORA_A]
            g_lo = p_ref[0, rows, 3 * D_MIX + LORA_W + LORA_A:]
            w = -_softplus(-(w0_ref[...] + _dot(jnp.tanh(lora_wa), w2_ref[...]))) - 0.5
            lw = -jnp.exp(w)
            a = _sigmoid(a0_ref[...] + _dot(lora_wa, a2_ref[...]))
            gate_s[rows, :] = _dot(_sigmoid(g_lo), g2_ref[...])
            kk = k * kk_ref[...]
            kk = kk * lax.rsqrt(_head_sums(kk * kk, e128) + L2_EPS)
            k_s[rows, :] = k * (1.0 + (a - 1.0) * ka_ref[...])
            aa_s[rows, :] = -kk
            bb_s[rows, :] = kk * a
            lw_s[rows, :] = lw
            lp_s[rows, :] = _sel_dot(lblk_ref[...], lw)
            yield

    def finish(chunks):
        inv_d = 1.0 / HEAD_DIM
        for c in chunks:
            rows = pl.ds(c * CHUNK, CHUNK)
            y = y_s[rows, :]
            yc = y - _head_sums(y, e128) * inv_d
            y = yc * lax.rsqrt(_head_sums(yc * yc, e128) * inv_d + GN_EPS) * lng_ref[...] + lnb_ref[...]
            r = p_ref[0, rows, 0:D_MIX]
            bonus = _head_sums(r * k_s[rows, :] * rk_ref[...], e128) * p_ref[0, rows, v_cols:v_cols + D_MIX]
            o_ref[0, rows, :] = (y + bonus) * gate_s[rows, :]
            yield

    def loads_of(chunks):
        loads = []
        for c in chunks:
            rows = pl.ds(c * CHUNK, CHUNK)
            for pr in range(N_PAIRS):
                cols = pl.ds(pr * PAIR, PAIR)
                r = p_ref[0, rows, cols]
                v = p_ref[0, rows, pl.ds(v_cols + pr * PAIR, PAIR)]
                loads.append((r, k_s[rows, cols], v, aa_s[rows, cols],
                              bb_s[rows, cols], lw_s[rows, cols], lp_s[rows, cols]))
        return loads

    hs = [state[pr] for pr in range(N_PAIRS)]

    def sequential(chunks):
        for c in chunks:
            rows = pl.ds(c * CHUNK, CHUNK)
            wh_rh = [jnp.dot(wr_s[c * N_PAIRS + pr], hs[pr].astype(BF16), preferred_element_type=F32)
                     for pr in range(N_PAIRS)]
            yield
            u = [(wh_rh[pr][:CHUNK] + ut_s[c * N_PAIRS + pr]).astype(BF16) for pr in range(N_PAIRS)]
            for pr in range(N_PAIRS):
                n = c * N_PAIRS + pr
                bu = jnp.dot(btt_s[n], u[pr], preferred_element_type=F32)
                hs[pr] = pc_s[n] * hs[pr] + jnp.where(same_head, bu, 0.0) + z0_s[n]
            yield
            for pr in range(N_PAIRS):
                n = c * N_PAIRS + pr
                y_s[rows, pl.ds(pr * PAIR, PAIR)] = (
                    wh_rh[pr][CHUNK:] + jnp.dot(arb_s[n], _bd(u[pr]), preferred_element_type=F32)
                    + y0_s[n])
            yield

    half = n_chunks // 2
    first, second = range(half), range(half, n_chunks)
    scratch = (wr_s, ut_s, arb_s, y0_s, btt_s, z0_s, pc_s)
    _drain(prepare(first))
    filler = prepare(second)
    _rwkv_precompute(loads_of(first), masks, *scratch, filler=filler)
    _drain(filler)
    filler = sequential(first)
    _rwkv_precompute(loads_of(second), masks, *scratch, base=half * N_PAIRS, filler=filler)
    _drain(filler)
    filler = finish(first)
    for _ in sequential(second):
        _advance(filler)
    _drain(filler)
    _drain(finish(second))
    for pr in range(N_PAIRS):
        state[pr] = hs[pr]


def _rwkv(p, params, consts):
    b, t, _ = p.shape
    tb = MIX_TB
    lblk, e128 = consts
    n_prob = (tb // CHUNK) * N_PAIRS
    const = lambda arr: pl.BlockSpec(arr.shape, lambda i, j: (0,) * arr.ndim)
    smalls = list(params) + [lblk, e128]
    return pl.pallas_call(
        _rwkv_body,
        grid=(b, t // tb),
        in_specs=[pl.BlockSpec((1, tb, RWKV_COLS), lambda i, j: (i, j, 0))] + [const(a) for a in smalls],
        out_specs=pl.BlockSpec((1, tb, D_MIX), lambda i, j: (i, j, 0)),
        out_shape=jax.ShapeDtypeStruct((b, t, D_MIX), F32),
        scratch_shapes=[pltpu.VMEM((tb, D_MIX), F32)] * 7
        + [pltpu.VMEM((N_PAIRS, PAIR, PAIR), F32),
           pltpu.VMEM((n_prob, 2 * CHUNK, PAIR), BF16),
           pltpu.VMEM((n_prob, CHUNK, PAIR), F32),
           pltpu.VMEM((n_prob, CHUNK, PAIR), BF16),
           pltpu.VMEM((n_prob, CHUNK, PAIR), F32),
           pltpu.VMEM((n_prob, PAIR, CHUNK), BF16),
           pltpu.VMEM((n_prob, PAIR, PAIR), F32),
           pltpu.VMEM((n_prob, PAIR, PAIR), F32)],
        compiler_params=pltpu.CompilerParams(
            dimension_semantics=("parallel", "arbitrary"), vmem_limit_bytes=VMEM_LIMIT),
        name="rwkv",
    )(p, *smalls)


def _selection_constants(tb):
    i = jnp.arange(tb)
    lblk = ((i[:, None] // CHUNK == i[None, :] // CHUNK) & (i[:, None] >= i[None, :])).astype(BF16)
    c = jnp.arange(PAIR)
    e128 = (c[:, None] // HEAD_DIM == c[None, :] // HEAD_DIM).astype(BF16)
    return lblk, e128


def kernel(x, ffn1_pre_g, ffn1_w_gate, ffn1_w_up, ffn1_w_down, ffn1_post_g, mix_pre_g, w_in, gdn_conv_w, gdn_a_log, gdn_dt_bias, gdn_norm_g, rwkv_mu, rwkv_w0, rwkv_w2, rwkv_a0, rwkv_a2, rwkv_g2, rwkv_k_k, rwkv_k_a, rwkv_r_k, rwkv_ln_g, rwkv_ln_b, w_out, mix_post_g, ffn2_pre_g, ffn2_w_gate, ffn2_w_up, ffn2_w_down, ffn2_post_g):
    b, t, d = x.shape
    depth = ffn1_pre_g.shape[0]
    consts = _selection_constants(CHUNK)
    rwkv_consts = _selection_constants(RWKV_PREP_ROWS)
    row = lambda v: v.reshape(1, -1).astype(F32)
    h = x.reshape(b * t, d)
    for l in range(depth):
        h = _ffn(h, row(ffn1_pre_g[l]), ffn1_w_gate[l].astype(BF16), ffn1_w_up[l].astype(BF16),
                 ffn1_w_down[l].astype(BF16), row(ffn1_post_g[l]))

        wl = w_in[l]
        n_qkvz = 4 * D_MIX
        w_cat = jnp.concatenate(
            [wl[:, :n_qkvz], wl[:, n_qkvz:n_qkvz + 2 * N_HEADS],
             jnp.zeros((d, GDN_SMALL - 2 * N_HEADS), wl.dtype), wl[:, n_qkvz + 2 * N_HEADS:]], axis=1)
        x_main, x_small, x_rwkv = _in_proj(h, row(mix_pre_g[l]), w_cat.astype(BF16),
                                           gdn_conv_w[l].astype(F32), row(rwkv_mu[l]), t)

        pad_small = lambda v: jnp.zeros((1, GDN_SMALL), F32).at[0, N_HEADS:2 * N_HEADS].set(v)
        y_gdn = _gdn(x_main.reshape(b, t, GDN_MAIN), x_small.reshape(b, t, GDN_SMALL),
                     pad_small(gdn_a_log[l]), pad_small(gdn_dt_bias[l]),
                     row(jnp.tile(gdn_norm_g[l], N_HEADS)), consts)

        zeros_lora = jnp.zeros((LORA_W, D_MIX), F32)
        w2_pad = jnp.concatenate([rwkv_w2[l], zeros_lora], axis=0).astype(BF16)
        a2_pad = jnp.concatenate([zeros_lora, rwkv_a2[l]], axis=0).astype(BF16)
        rwkv_params = (row(rwkv_w0[l]), row(rwkv_a0[l]), w2_pad, a2_pad,
                       rwkv_g2[l].astype(BF16), row(rwkv_k_k[l]), row(rwkv_k_a[l]), row(rwkv_r_k[l]),
                       row(rwkv_ln_g[l]), row(rwkv_ln_b[l]))
        y_rwkv = _rwkv(x_rwkv.reshape(b, t, RWKV_COLS), rwkv_params, rwkv_consts)

        h = _ffn(h, row(ffn2_pre_g[l]), ffn2_w_gate[l].astype(BF16), ffn2_w_up[l].astype(BF16),
                 ffn2_w_down[l].astype(BF16), row(ffn2_post_g[l]),
                 mix=(y_gdn.reshape(b * t, D_MIX), y_rwkv.reshape(b * t, D_MIX),
                      w_out[l].astype(BF16), row(mix_post_g[l])))
    return h.reshape(b, t, d)
```
